```python
import jax, jax.numpy as jnp
from jax import lax
import numpy as np

D_MODEL = 2048
BATCH = 4
SEQ = 4096
DEPTH = 1

HEAD_DIM = 128
N_ATT_HEADS = D_MODEL // (2 * HEAD_DIM)
N_RET_HEADS = D_MODEL // (2 * HEAD_DIM)
D_ATT = N_ATT_HEADS * HEAD_DIM
D_RET = N_RET_HEADS * HEAD_DIM
D_MIX = D_ATT + D_RET
D_IN_PROJ = 3 * D_ATT + 4 * D_RET
DILATED_PATTERNS = ((128, 1), (512, 4), (2048, 16))
RET_CHUNK = 128
D_FF = -(-(8 * D_MODEL) // (3 * 256)) * 256
EPS = 1e-6

kernel_name = 'hybrid_dilated_attn_retention_block'


def rmsnorm(x, w):
    xf = x.astype(jnp.float32)
    xf = xf * lax.rsqrt(jnp.mean(xf * xf, axis=-1, keepdims=True) + EPS)
    return (xf * w.astype(jnp.float32)).astype(x.dtype)


def alibi_slopes(n_heads):
    return jnp.exp2(-8.0 * jnp.arange(1, n_heads + 1, dtype=jnp.float32) / n_heads)


def dilated_window_partial(q, k, v, slopes, window, dilation):
    B, H, S, Dh = q.shape
    half = window // (2 * dilation)
    blk = half
    L = S // dilation
    nb = -(-L // blk)
    Lp = nb * blk

    def to_sub(t):
        return t.reshape(B, H, L, dilation, Dh).transpose(0, 1, 3, 2, 4)

    qb = jnp.pad(to_sub(q), ((0, 0), (0, 0), (0, 0), (0, Lp - L), (0, 0)))
    qb = qb.reshape(B, H, dilation, nb, blk, Dh)

    def key_windows(t):
        tp = jnp.pad(to_sub(t), ((0, 0), (0, 0), (0, 0), (blk, Lp - L + blk), (0, 0)))
        tb = tp.reshape(B, H, dilation, nb + 2, blk, Dh)
        return jnp.concatenate([tb[:, :, :, :-2], tb[:, :, :, 1:-1], tb[:, :, :, 2:]], axis=4)

    kw = key_windows(k)
    vw = key_windows(v)
    s = jnp.einsum('bhrnqd,bhrnkd->bhrnqk', qb, kw) * (Dh ** -0.5)
    lq = jnp.arange(nb)[:, None] * blk + jnp.arange(blk)[None, :]
    lk = jnp.arange(nb)[:, None] * blk - blk + jnp.arange(3 * blk)[None, :]
    dist = jnp.abs(lq[:, :, None] - lk[:, None, :])
    valid = (dist <= half) & (lk[:, None, :] >= 0) & (lk[:, None, :] < L)
    bias = -slopes[:, None, None, None, None] * (dilation * dist).astype(jnp.float32)
    s = jnp.where(valid, s + bias, -jnp.inf)
    m = jnp.max(s, axis=-1)
    p = jnp.exp(s - m[..., None])
    den = jnp.sum(p, axis=-1)
    num = jnp.einsum('bhrnqk,bhrnkd->bhrnqd', p, vw)
    num = num.reshape(B, H, dilation, Lp, Dh)[:, :, :, :L].transpose(0, 1, 3, 2, 4).reshape(B, H, S, Dh)
    m = m.reshape(B, H, dilation, Lp)[..., :L].transpose(0, 1, 3, 2).reshape(B, H, S)
    den = den.reshape(B, H, dilation, Lp)[..., :L].transpose(0, 1, 3, 2).reshape(B, H, S)
    return num, m, den


def dilated_mixture_attention(q, k, v, slopes):
    q, k, v = (t.astype(jnp.float32) for t in (q, k, v))
    parts = [dilated_window_partial(q, k, v, slopes, w, d) for (w, d) in DILATED_PATTERNS]
    m_all = jnp.max(jnp.stack([p[1] for p in parts], axis=0), axis=0)
    num = 0.0
    den = 0.0
    for (n_i, m_i, d_i) in parts:
        w_i = jnp.exp(m_i - m_all)
        num = num + w_i[..., None] * n_i
        den = den + w_i * d_i
    return num / den[..., None]


def retention_direction(q, k, v, log_gamma, strict):
    B, H, S, Dh = q.shape
    C = RET_CHUNK
    nc = S // C
    idx = jnp.arange(C, dtype=jnp.float32)
    rel = idx[:, None] - idx[None, :]
    inside = (rel > 0) if strict else (rel >= 0)
    decay_mask = jnp.where(inside, jnp.exp(log_gamma[:, None, None] * jnp.maximum(rel, 0.0)), 0.0)
    q_dec = jnp.exp(log_gamma[:, None] * (idx + 1.0))[..., None]
    k_dec = jnp.exp(log_gamma[:, None] * (C - 1.0 - idx))[..., None]
    chunk_dec = jnp.exp(log_gamma * C)[:, None, None]

    def to_chunks(t):
        return t.reshape(B, H, nc, C, Dh).transpose(2, 0, 1, 3, 4)

    def step(state, qkv):
        qc, kc, vc = qkv
        inner = jnp.einsum('bhid,bhjd->bhij', qc, kc) * decay_mask
        o = jnp.einsum('bhij,bhjd->bhid', inner, vc) + jnp.einsum('bhid,bhde->bhie', qc * q_dec, state)
        state = state * chunk_dec + jnp.einsum('bhjd,bhje->bhde', kc * k_dec, vc)
        return state, o

    state0 = jnp.zeros((B, H, Dh, Dh), jnp.float32)
    _, o = lax.scan(step, state0, (to_chunks(q), to_chunks(k), to_chunks(v)))
    return o.transpose(1, 2, 0, 3, 4).reshape(B, H, S, Dh)


def bidirectional_retention(q, k, v, decay_fwd, decay_bwd):
    q, k, v = (t.astype(jnp.float32) for t in (q, k, v))
    q = q * (q.shape[-1] ** -0.5)
    lg_f = -jnp.exp(decay_fwd.astype(jnp.float32))
    lg_b = -jnp.exp(decay_bwd.astype(jnp.float32))
    o_f = retention_direction(q, k, v, lg_f, strict=False)
    flip = lambda t: jnp.flip(t, axis=2)
    o_b = flip(retention_direction(flip(q), flip(k), flip(v), lg_b, strict=True))
    return o_f + o_b


def setup_inputs(seed: int = 0) -> dict:
    key = jax.random.key(seed)
    ks = jax.random.split(key, 16)
    f32 = jnp.float32
    nrm = lambda k, shape, scale: jax.random.normal(k, shape, f32) * scale
    gain = lambda k, shape: 1.0 + 0.01 * jax.random.normal(k, shape, f32)
    base = np.log(-np.log(1.0 - 2.0 ** (-5.0 - np.arange(N_RET_HEADS)))).astype(np.float32)
    base = jnp.asarray(base)[None, :]
    return {
        'x': jax.random.normal(ks[0], (BATCH, SEQ, D_MODEL), f32),
        'norm_mix_w': gain(ks[1], (DEPTH, D_MODEL)),
        'w_in': nrm(ks[2], (DEPTH, D_MODEL, D_IN_PROJ), D_MODEL ** -0.5),
        'ret_decay_fwd': base + 0.05 * jax.random.normal(ks[3], (DEPTH, N_RET_HEADS), f32),
        'ret_decay_bwd': base + 0.05 * jax.random.normal(ks[4], (DEPTH, N_RET_HEADS), f32),
        'ret_norm_w': gain(ks[5], (DEPTH, D_RET)),
        'w_out': nrm(ks[6], (DEPTH, D_MIX, D_MODEL), D_MIX ** -0.5),
        'norm_ffn_w': gain(ks[7], (DEPTH, D_MODEL)),
        'w_gate': nrm(ks[8], (DEPTH, D_MODEL, D_FF), D_MODEL ** -0.5),
        'w_up': nrm(ks[9], (DEPTH, D_MODEL, D_FF), D_MODEL ** -0.5),
        'w_down': nrm(ks[10], (DEPTH, D_FF, D_MODEL), D_FF ** -0.5),
        'norm_final_w': gain(ks[11], (D_MODEL,)),
    }


def reference(x, norm_mix_w, w_in, ret_decay_fwd, ret_decay_bwd, ret_norm_w, w_out,
              norm_ffn_w, w_gate, w_up, w_down, norm_final_w):
    B, S, _ = x.shape
    slopes = alibi_slopes(N_ATT_HEADS)
    splits = [D_ATT, 2 * D_ATT, 3 * D_ATT, 3 * D_ATT + D_RET, 3 * D_ATT + 2 * D_RET, 3 * D_ATT + 3 * D_RET]

    def heads(t):
        return t.reshape(B, S, -1, HEAD_DIM).transpose(0, 2, 1, 3)

    def merge(t):
        return t.transpose(0, 2, 1, 3).reshape(B, S, -1)

    h = x
    for layer in range(DEPTH):
        n = rmsnorm(h, norm_mix_w[layer])
        proj = n @ w_in[layer]
        q_a, k_a, v_a, q_r, k_r, v_r, g_r = jnp.split(proj, splits, axis=-1)
        attn = dilated_mixture_attention(heads(q_a), heads(k_a), heads(v_a), slopes)
        ret = bidirectional_retention(heads(q_r), heads(k_r), heads(v_r),
                                      ret_decay_fwd[layer], ret_decay_bwd[layer])
        ret = ret * lax.rsqrt(jnp.mean(ret * ret, axis=-1, keepdims=True) + EPS)
        ret = merge(ret) * ret_norm_w[layer].astype(jnp.float32)
        ret = ret * jax.nn.silu(g_r.astype(jnp.float32))
        mixed = jnp.concatenate([merge(attn), ret], axis=-1).astype(x.dtype)
        h = h + mixed @ w_out[layer]
        n2 = rmsnorm(h, norm_ffn_w[layer])
        h = h + (jax.nn.silu(n2 @ w_gate[layer]) * (n2 @ w_up[layer])) @ w_down[layer]
    return rmsnorm(h, norm_final_w)
```

```python
import functools

import jax
import jax.numpy as jnp
from jax import lax
from jax.experimental import pallas as pl
from jax.experimental.pallas import tpu as pltpu

F32 = jnp.float32
BF16 = jnp.bfloat16

HEAD_DIM = 128
N_HEADS = 8
D_GROUP = N_HEADS * HEAD_DIM
DILATED_PATTERNS = ((128, 1), (512, 4), (2048, 16))
HALF_BAND = 64
RET_CHUNK = 128
EPS = 1e-6
NEG_BIG = -1e30
MIB = 1024 * 1024

ATT_TQ = 128
ATT_TK = ATT_TQ + 2 * HALF_BAND


def _params(semantics, vmem_mib):
    return pltpu.CompilerParams(dimension_semantics=semantics,
                                vmem_limit_bytes=vmem_mib * MIB)


def _rmsnorm_rows(x, w):
    return x * lax.rsqrt(jnp.mean(x * x, axis=-1, keepdims=True) + EPS) * w


def _inproj_kernel(x_ref, nw_ref, w_ref, o_ref, n_scr):
    @pl.when(pl.program_id(1) == 0)
    def _():
        n_scr[...] = _rmsnorm_rows(x_ref[...], nw_ref[...]).astype(BF16)

    res = jnp.dot(n_scr[...], w_ref[...], preferred_element_type=F32)
    for h in range(N_HEADS):
        o_ref[h] = res[:, h * HEAD_DIM:(h + 1) * HEAD_DIM].astype(o_ref.dtype)


def _inproj(x2, norm_w, w, batch, seq, out_dtype, tm=1024):
    n_tok, d_model = x2.shape
    groups = w.shape[1] // D_GROUP
    s_blocks = seq // tm
    return pl.pallas_call(
        _inproj_kernel,
        grid=(n_tok // tm, groups),
        in_specs=[
            pl.BlockSpec((tm, d_model), lambda i, j: (i, 0)),
            pl.BlockSpec((1, d_model), lambda i, j: (0, 0)),
            pl.BlockSpec((d_model, D_GROUP), lambda i, j: (0, j)),
        ],
        out_specs=pl.BlockSpec(
            (None, None, N_HEADS, tm, HEAD_DIM),
            lambda i, j: (j, i // s_blocks, 0, i % s_blocks, 0)),
        out_shape=jax.ShapeDtypeStruct(
            (groups, batch, N_HEADS, seq, HEAD_DIM), out_dtype),
        scratch_shapes=[pltpu.VMEM((tm, d_model), BF16)],
        compiler_params=_params(("parallel", "arbitrary"), 48),
        name="inproj",
    )(x2, norm_w, w)


def _attn_kernel(slopes_ref, q_ref, k_ref, v_ref, o_ref,
                 qs, ks, vs, acc, m_s, l_s, bias_s):
    seq = q_ref.shape[0]
    slope = slopes_ref[pl.program_id(1)]
    scale = HEAD_DIM ** -0.5

    row = lax.broadcasted_iota(jnp.int32, (ATT_TQ, ATT_TK), 0)
    col = lax.broadcasted_iota(jnp.int32, (ATT_TQ, ATT_TK), 1)
    for p, (_, dil) in enumerate(DILATED_PATTERNS):
        for c, off in enumerate((0, HALF_BAND, 2 * HALF_BAND)):
            dist = jnp.abs(row - col + off)
            bias = -slope * (dil * dist).astype(F32)
            bias_s[p * 3 + c] = jnp.where(dist <= HALF_BAND, bias, NEG_BIG)

    for p, (_, dil) in enumerate(DILATED_PATTERNS):
        sub_len = seq // dil
        n_blk = sub_len // ATT_TQ
        for src, dst in ((q_ref, qs), (k_ref, ks), (v_ref, vs)):
            if dil == 1:
                dst[...] = src[...].astype(BF16)
            else:
                for r in range(dil):
                    dst[r * sub_len:(r + 1) * sub_len, :] = (
                        src[pl.ds(r, sub_len, stride=dil), :].astype(BF16))

        def block(idx, carry, p=p, dil=dil, sub_len=sub_len, n_blk=n_blk):
            r = idx // n_blk
            blk = idx % n_blk
            base = r * sub_len
            q0 = blk * ATT_TQ
            k0 = jnp.clip(q0 - HALF_BAND, 0, sub_len - ATT_TK)
            q = qs[pl.ds(pl.multiple_of(base + q0, ATT_TQ), ATT_TQ), :]
            k = ks[pl.ds(pl.multiple_of(base + k0, HALF_BAND), ATT_TK), :]
            v = vs[pl.ds(pl.multiple_of(base + k0, HALF_BAND), ATT_TK), :]
            case = jnp.where(blk == 0, 0, jnp.where(blk == n_blk - 1, 2, 1))
            s = lax.dot_general(q, k, (((1,), (1,)), ((), ())),
                                preferred_element_type=F32)
            s = s * scale + bias_s[p * 3 + case]
            m_blk = jnp.max(s, axis=-1, keepdims=True)
            if dil == 1:
                rows = pl.ds(pl.multiple_of(q0, ATT_TQ), ATT_TQ)
            else:
                rows = pl.ds(r + dil * q0, ATT_TQ, stride=dil)
            if p == 0:
                m_new = jnp.broadcast_to(m_blk, (ATT_TQ, HEAD_DIM))
            else:
                m_prev = m_s[rows, :]
                m_new = jnp.maximum(m_prev, m_blk)
            e = jnp.exp(s - jnp.concatenate([m_new, m_new], axis=-1))
            l_blk = jnp.sum(e, axis=-1, keepdims=True)
            pv = jnp.dot(e.astype(BF16), v, preferred_element_type=F32)
            if p == 0:
                l_s[rows, :] = jnp.broadcast_to(l_blk, (ATT_TQ, HEAD_DIM))
                acc[rows, :] = pv
            else:
                alpha = jnp.exp(m_prev - m_new)
                l_s[rows, :] = alpha * l_s[rows, :] + l_blk
                acc[rows, :] = alpha * acc[rows, :] + pv
            m_s[rows, :] = m_new
            return carry

        lax.fori_loop(0, dil * n_blk, block, 0)

    o_ref[...] = (acc[...] / l_s[...]).astype(o_ref.dtype)


def _attention(slopes, qkv):
    _, batch, heads, seq, dh = qkv.shape
    assert seq % (16 * ATT_TK) == 0 and dh == HEAD_DIM

    def spec(t):
        return pl.BlockSpec((None, None, None, seq, dh),
                            lambda b, h, t=t: (t, b, h, 0, 0))

    return pl.pallas_call(
        _attn_kernel,
        grid=(batch, heads),
        in_specs=[pl.BlockSpec(memory_space=pltpu.SMEM), spec(0), spec(1), spec(2)],
        out_specs=pl.BlockSpec((None, None, seq, dh), lambda b, h: (b, h, 0, 0)),
        out_shape=jax.ShapeDtypeStruct((batch, heads, seq, dh), BF16),
        scratch_shapes=[
            pltpu.VMEM((seq, dh), BF16), pltpu.VMEM((seq, dh), BF16),
            pltpu.VMEM((seq, dh), BF16),
            pltpu.VMEM((seq, dh), F32), pltpu.VMEM((seq, dh), F32),
            pltpu.VMEM((seq, dh), F32),
            pltpu.VMEM((3 * len(DILATED_PATTERNS), ATT_TQ, ATT_TK), F32),
        ],
        compiler_params=_params(("parallel", "parallel"), 40),
        name="dilated_attention",
    )(slopes, qkv, qkv, qkv)


def _ret_kernel(df_ref, db_ref, q_ref, k_ref, v_ref, g_ref, rw_ref, o_ref, ob_s):
    seq = q_ref.shape[0]
    ch = RET_CHUNK
    n_ch = seq // ch
    h = pl.program_id(1)
    scale = HEAD_DIM ** -0.5
    sq = (ch, HEAD_DIM)

    lg_f = -jnp.exp(jnp.full(sq, df_ref[h], F32))
    lg_b = -jnp.exp(jnp.full(sq, db_ref[h], F32))
    ri = lax.broadcasted_iota(jnp.int32, sq, 0).astype(F32)
    ci = lax.broadcasted_iota(jnp.int32, sq, 1).astype(F32)
    rel = ri - ci
    intra = jnp.where(rel >= 0.0,
                      jnp.exp(lg_f * jnp.maximum(rel, 0.0)),
                      jnp.exp(lg_b * jnp.maximum(-rel, 0.0))) * scale
    qdec_f = jnp.exp(lg_f * (ri + 1.0)) * scale
    kdec_f = jnp.exp(lg_f * (ch - 1.0 - ri))
    cdec_f = jnp.exp(lg_f * ch)
    qdec_b = jnp.exp(lg_b * (ch - ri)) * scale
    kdec_b = jnp.exp(lg_b * ri)
    cdec_b = jnp.exp(lg_b * ch)

    def state_update(state, cdec, k_dec, vc):
        kv = lax.dot_general(k_dec.astype(BF16), vc, (((0,), (0,)), ((), ())),
                             preferred_element_type=F32)
        return state * cdec + kv

    def bwd(i, state):
        rows = pl.ds(pl.multiple_of((n_ch - 1 - i) * ch, ch), ch)
        qc = q_ref[rows, :].astype(F32)
        kc = k_ref[rows, :].astype(F32)
        vc = v_ref[rows, :]
        ob_s[rows, :] = jnp.dot((qc * qdec_b).astype(BF16), state.astype(BF16),
                                preferred_element_type=F32)
        return state_update(state, cdec_b, kc * kdec_b, vc)

    lax.fori_loop(0, n_ch, bwd, jnp.zeros(sq, F32))

    def fwd(i, state):
        rows = pl.ds(pl.multiple_of(i * ch, ch), ch)
        qb = q_ref[rows, :]
        kb = k_ref[rows, :]
        vc = v_ref[rows, :]
        qc = qb.astype(F32)
        inner = lax.dot_general(qb, kb, (((1,), (1,)), ((), ())),
                                preferred_element_type=F32) * intra
        o = jnp.dot(inner.astype(BF16), vc, preferred_element_type=F32)
        o = o + jnp.dot((qc * qdec_f).astype(BF16), state.astype(BF16),
                        preferred_element_type=F32)
        o = o + ob_s[rows, :]
        o = o * lax.rsqrt(jnp.mean(o * o, axis=-1, keepdims=True) + EPS)
        o = o * rw_ref[...]
        g = g_ref[rows, :].astype(F32)
        o = o * (g * (1.0 / (1.0 + jnp.exp(-g))))
        o_ref[rows, :] = o.astype(o_ref.dtype)
        return state_update(state, cdec_f, kb.astype(F32) * kdec_f, vc)

    lax.fori_loop(0, n_ch, fwd, jnp.zeros(sq, F32))


def _retention(decay_f, decay_b, qkvg, ret_norm_w):
    _, batch, heads, seq, dh = qkvg.shape

    def spec(t):
        return pl.BlockSpec((None, None, None, seq, dh),
                            lambda b, h, t=t: (t, b, h, 0, 0))

    smem = pl.BlockSpec(memory_space=pltpu.SMEM)
    return pl.pallas_call(
        _ret_kernel,
        grid=(batch, heads),
        in_specs=[smem, smem, spec(0), spec(1), spec(2), spec(3),
                  pl.BlockSpec((None, 1, dh), lambda b, h: (h, 0, 0))],
        out_specs=pl.BlockSpec((None, None, seq, dh), lambda b, h: (b, h, 0, 0)),
        out_shape=jax.ShapeDtypeStruct((batch, heads, seq, dh), BF16),
        scratch_shapes=[pltpu.VMEM((seq, dh), F32)],
        compiler_params=_params(("parallel", "parallel"), 32),
        name="retention",
    )(decay_f, decay_b, qkvg, qkvg, qkvg, qkvg, ret_norm_w)


def _outproj_kernel(a_ref, r_ref, x_ref, w_ref, nw_ref, h_ref, n_ref):
    mixed = jnp.concatenate([a_ref[h] for h in range(N_HEADS)]
                            + [r_ref[h] for h in range(N_HEADS)], axis=-1)
    hid = x_ref[...] + jnp.dot(mixed, w_ref[...], preferred_element_type=F32)
    h_ref[...] = hid
    n_ref[...] = _rmsnorm_rows(hid, nw_ref[...]).astype(n_ref.dtype)


def _outproj(attn, ret, x2, w_out, norm_w, tm=512):
    batch, heads, seq, dh = attn.shape
    n_tok, d_model = x2.shape
    s_blocks = seq // tm
    head_spec = pl.BlockSpec((None, heads, tm, dh),
                             lambda i: (i // s_blocks, 0, i % s_blocks, 0))
    row_spec = pl.BlockSpec((tm, d_model), lambda i: (i, 0))
    return pl.pallas_call(
        _outproj_kernel,
        grid=(n_tok // tm,),
        in_specs=[head_spec, head_spec, row_spec,
                  pl.BlockSpec(w_out.shape, lambda i: (0, 0)),
                  pl.BlockSpec((1, d_model), lambda i: (0, 0))],
        out_specs=[row_spec, row_spec],
        out_shape=[jax.ShapeDtypeStruct((n_tok, d_model), F32),
                   jax.ShapeDtypeStruct((n_tok, d_model), BF16)],
        compiler_params=_params(("parallel",), 56),
        name="outproj",
    )(attn, ret, x2, w_out, norm_w)


def _ffn_kernel(n_ref, h_ref, wg_ref, wu_ref, wd_ref, fw_ref, o_ref, acc, *, final_norm):
    f = pl.program_id(1)
    n = n_ref[...]
    g = jnp.dot(n, wg_ref[...], preferred_element_type=F32)
    u = jnp.dot(n, wu_ref[...], preferred_element_type=F32)
    a = (g * (1.0 / (1.0 + jnp.exp(-g))) * u).astype(BF16)
    part = jnp.dot(a, wd_ref[...], preferred_element_type=F32)

    @pl.when(f == 0)
    def _():
        acc[...] = h_ref[...] + part

    @pl.when(f > 0)
    def _():
        acc[...] += part

    @pl.when(f == pl.num_programs(1) - 1)
    def _():
        out = acc[...]
        if final_norm:
            out = _rmsnorm_rows(out, fw_ref[...])
        o_ref[...] = out


def _ffn(n2, hid, w_gate, w_up, w_down, final_w, final_norm, tm=512, tf=512):
    n_tok, d_model = hid.shape
    d_ff = w_gate.shape[1]
    row = lambda i, f: (i, 0)
    return pl.pallas_call(
        functools.partial(_ffn_kernel, final_norm=final_norm),
        grid=(n_tok // tm, d_ff // tf),
        in_specs=[pl.BlockSpec((tm, d_model), row),
                  pl.BlockSpec((tm, d_model), row),
                  pl.BlockSpec((d_model, tf), lambda i, f: (0, f)),
                  pl.BlockSpec((d_model, tf), lambda i, f: (0, f)),
                  pl.BlockSpec((tf, d_model), lambda i, f: (f, 0)),
                  pl.BlockSpec((1, d_model), lambda i, f: (0, 0))],
        out_specs=pl.BlockSpec((tm, d_model), row),
        out_shape=jax.ShapeDtypeStruct((n_tok, d_model), F32),
        scratch_shapes=[pltpu.VMEM((tm, d_model), F32)],
        compiler_params=_params(("parallel", "arbitrary"), 48),
        name="ffn",
    )(n2, hid, w_gate, w_up, w_down, final_w)


def kernel(x, norm_mix_w, w_in, ret_decay_fwd, ret_decay_bwd, ret_norm_w, w_out,
           norm_ffn_w, w_gate, w_up, w_down, norm_final_w):
    batch, seq, d_model = x.shape
    depth = w_in.shape[0]
    assert depth >= 1
    d_att = 3 * D_GROUP
    slopes = jnp.exp2(-8.0 * jnp.arange(1, N_HEADS + 1, dtype=F32) / N_HEADS)
    final_w = norm_final_w.reshape(1, d_model)

    hid = x.reshape(batch * seq, d_model)
    for layer in range(depth):
        w_in_l = w_in[layer]
        nw = norm_mix_w[layer].reshape(1, d_model)
        qkv_a = _inproj(hid, nw, w_in_l[:, :d_att].astype(BF16), batch, seq, F32)
        qkvg_r = _inproj(hid, nw, w_in_l[:, d_att:].astype(BF16), batch, seq, BF16)
        attn = _attention(slopes, qkv_a)
        ret = _retention(ret_decay_fwd[layer], ret_decay_bwd[layer], qkvg_r,
                         ret_norm_w[layer].reshape(N_HEADS, 1, HEAD_DIM))
        hid, n2 = _outproj(attn, ret, hid, w_out[layer].astype(BF16),
                           norm_ffn_w[layer].reshape(1, d_model))
        last = layer == depth - 1
        hid = _ffn(n2, hid, w_gate[layer].astype(BF16), w_up[layer].astype(BF16),
                   w_down[layer].astype(BF16), final_w, final_norm=last)
    return hid.reshape(batch, seq, d_model)
```

```python
import functools

import jax
import jax.numpy as jnp
from jax import lax
from jax.experimental import pallas as pl
from jax.experimental.pallas import tpu as pltpu

F32 = jnp.float32
BF16 = jnp.bfloat16

HEAD_DIM = 128
N_HEADS = 8
D_GROUP = N_HEADS * HEAD_DIM
DILATED_PATTERNS = ((128, 1), (512, 4), (2048, 16))
HALF_BAND = 64
RET_CHUNK = 128
EPS = 1e-6
NEG_BIG = -1e30
MIB = 1024 * 1024

ATT_TQ = 128
ATT_TK = ATT_TQ + 2 * HALF_BAND


def _params(semantics, vmem_mib):
    return pltpu.CompilerParams(dimension_semantics=semantics,
                                vmem_limit_bytes=vmem_mib * MIB)


def _rmsnorm_rows(x, w):
    return x * lax.rsqrt(jnp.mean(x * x, axis=-1, keepdims=True) + EPS) * w


def _inproj_kernel(x_ref, nw_ref, w_ref, o_ref, n_scr):
    @pl.when(pl.program_id(1) == 0)
    def _():
        n_scr[...] = _rmsnorm_rows(x_ref[...], nw_ref[...]).astype(BF16)

    res = jnp.dot(n_scr[...], w_ref[...], preferred_element_type=F32)
    for h in range(N_HEADS):
        o_ref[h] = res[:, h * HEAD_DIM:(h + 1) * HEAD_DIM].astype(o_ref.dtype)


def _inproj(x2, norm_w, w, batch, seq, out_dtype, tm=1024):
    n_tok, d_model = x2.shape
    groups = w.shape[1] // D_GROUP
    s_blocks = seq // tm
    return pl.pallas_call(
        _inproj_kernel,
        grid=(n_tok // tm, groups),
        in_specs=[
            pl.BlockSpec((tm, d_model), lambda i, j: (i, 0)),
            pl.BlockSpec((1, d_model), lambda i, j: (0, 0)),
            pl.BlockSpec((d_model, D_GROUP), lambda i, j: (0, j)),
        ],
        out_specs=pl.BlockSpec(
            (None, None, N_HEADS, tm, HEAD_DIM),
            lambda i, j: (j, i // s_blocks, 0, i % s_blocks, 0)),
        out_shape=jax.ShapeDtypeStruct(
            (groups, batch, N_HEADS, seq, HEAD_DIM), out_dtype),
        scratch_shapes=[pltpu.VMEM((tm, d_model), BF16)],
        compiler_params=_params(("parallel", "arbitrary"), 48),
        name="inproj",
    )(x2, norm_w, w)


def _attn_kernel(slopes_ref, q_ref, k_ref, v_ref, o_ref,
                 q4f, k4f, v4f, qs, ks, vs,
                 num4, den4, m4, num1, den1, m1, bias_s):
    seq = q_ref.shape[0]
    quarter = seq // 4
    slope = slopes_ref[pl.program_id(1)]
    scale = HEAD_DIM ** -0.5

    row = lax.broadcasted_iota(jnp.int32, (ATT_TQ, ATT_TK), 0)
    col = lax.broadcasted_iota(jnp.int32, (ATT_TQ, ATT_TK), 1)
    for p, (_, dil) in enumerate(DILATED_PATTERNS):
        for c, off in enumerate((0, HALF_BAND, 2 * HALF_BAND)):
            dist = jnp.abs(row - col + off)
            bias = -slope * (dil * dist).astype(F32)
            bias_s[p * 3 + c] = jnp.where(dist <= HALF_BAND, bias, NEG_BIG)

    vs[:, HEAD_DIM:] = jnp.ones((seq, HEAD_DIM), BF16)

    for src, dst in ((q_ref, q4f), (k_ref, k4f), (v_ref, v4f)):
        for r in range(4):
            dst[r * quarter:(r + 1) * quarter, :] = src[pl.ds(r, quarter, stride=4), :]

    def run_pattern(p, dil, incoming, emit):
        sub_len = seq // dil
        n_blk = sub_len // ATT_TQ
        shift = n_blk.bit_length() - 1
        assert n_blk == 1 << shift and n_blk >= 2

        def block(idx, carry):
            r = idx >> shift
            blk = idx & (n_blk - 1)
            q0 = blk * ATT_TQ
            k0 = jnp.clip(q0 - HALF_BAND, 0, sub_len - ATT_TK)
            q_rows = pl.ds(pl.multiple_of(r * sub_len + q0, ATT_TQ), ATT_TQ)
            k_rows = pl.ds(pl.multiple_of(r * sub_len + k0, HALF_BAND), ATT_TK)
            case = jnp.where(blk == 0, 0, jnp.where(blk == n_blk - 1, 2, 1))
            s = lax.dot_general(qs[q_rows, :], ks[k_rows, :], (((1,), (1,)), ((), ())),
                                preferred_element_type=F32)
            s = s * scale + bias_s[p * 3 + case]
            m_blk = jnp.max(s, axis=-1, keepdims=True)
            if incoming is None:
                m_new = jnp.broadcast_to(m_blk, (ATT_TQ, HEAD_DIM))
            else:
                num_in, den_in, m_in = incoming
                m_old = m_in[q_rows, :]
                m_new = jnp.maximum(m_old, m_blk)
            e = jnp.exp(s - jnp.concatenate([m_new, m_new], axis=-1))
            pv = jnp.dot(e.astype(BF16), vs[k_rows, :], preferred_element_type=F32)
            num = pv[:, :HEAD_DIM]
            den = pv[:, HEAD_DIM:]
            if incoming is not None:
                alpha = jnp.exp(m_old - m_new)
                num = alpha * num_in[q_rows, :] + num
                den = alpha * den_in[q_rows, :] + den
            emit(r, q0, num, den, m_new)
            return carry

        lax.fori_loop(0, dil * n_blk, block, 0, unroll=16)

    sub16 = seq // 16
    for src, dst in ((q4f, qs), (k4f, ks), (v4f, vs)):
        for r16 in range(16):
            r4, b = r16 % 4, r16 // 4
            dst[r16 * sub16:(r16 + 1) * sub16, :HEAD_DIM] = (
                src[pl.ds(r4 * quarter + b, sub16, stride=4), :].astype(BF16))

    def emit16(r, q0, num, den, m_new):
        rows = pl.ds((r & 3) * quarter + 4 * q0 + (r >> 2), ATT_TQ, stride=4)
        num4[rows, :] = num
        den4[rows, :] = den
        m4[rows, :] = m_new

    run_pattern(2, 16, None, emit16)

    for src, dst in ((q4f, qs), (k4f, ks), (v4f, vs)):
        dst[:, :HEAD_DIM] = src[...].astype(BF16)

    def emit4(r, q0, num, den, m_new):
        rows = pl.ds(4 * q0 + r, ATT_TQ, stride=4)
        num1[rows, :] = num
        den1[rows, :] = den
        m1[rows, :] = m_new

    run_pattern(1, 4, (num4, den4, m4), emit4)

    for src, dst in ((q_ref, qs), (k_ref, ks), (v_ref, vs)):
        dst[:, :HEAD_DIM] = src[...].astype(BF16)

    def emit1(r, q0, num, den, m_new):
        o_ref[pl.ds(pl.multiple_of(q0, ATT_TQ), ATT_TQ), :] = (num / den).astype(o_ref.dtype)

    run_pattern(0, 1, (num1, den1, m1), emit1)


def _attention(slopes, qkv):
    _, batch, heads, seq, dh = qkv.shape
    assert seq % (16 * ATT_TK) == 0 and dh == HEAD_DIM

    def spec(t):
        return pl.BlockSpec((None, None, None, seq, dh),
                            lambda b, h, t=t: (t, b, h, 0, 0))

    f32_rows = pltpu.VMEM((seq, dh), F32)
    return pl.pallas_call(
        _attn_kernel,
        grid=(batch, heads),
        in_specs=[pl.BlockSpec(memory_space=pltpu.SMEM), spec(0), spec(1), spec(2)],
        out_specs=pl.BlockSpec((None, None, seq, dh), lambda b, h: (b, h, 0, 0)),
        out_shape=jax.ShapeDtypeStruct((batch, heads, seq, dh), BF16),
        scratch_shapes=[
            f32_rows, f32_rows, f32_rows,
            pltpu.VMEM((seq, dh), BF16), pltpu.VMEM((seq, dh), BF16),
            pltpu.VMEM((seq, 2 * dh), BF16),
            f32_rows, f32_rows, f32_rows, f32_rows, f32_rows, f32_rows,
            pltpu.VMEM((3 * len(DILATED_PATTERNS), ATT_TQ, ATT_TK), F32),
        ],
        compiler_params=_params(("parallel", "parallel"), 48),
        name="dilated_attention",
    )(slopes, qkv, qkv, qkv)


def _ret_kernel(df_ref, db_ref, q_ref, k_ref, v_ref, g_ref, rw_ref, o_ref,
                tab, kv_s, st_s):
    seq = q_ref.shape[0]
    ch = RET_CHUNK
    n_ch = seq // ch
    h = pl.program_id(1)
    scale = HEAD_DIM ** -0.5
    sq = (ch, HEAD_DIM)
    T_INTRA, T_QF, T_QB, T_KF, T_KB = range(5)

    lg_f = -jnp.exp(jnp.full((1, HEAD_DIM), df_ref[h], F32))
    lg_b = -jnp.exp(jnp.full((1, HEAD_DIM), db_ref[h], F32))
    ri = lax.broadcasted_iota(jnp.int32, sq, 0).astype(F32)
    ci = lax.broadcasted_iota(jnp.int32, sq, 1).astype(F32)
    rel = ri - ci
    tab[T_INTRA] = jnp.where(rel >= 0.0,
                             jnp.exp(lg_f * jnp.maximum(rel, 0.0)),
                             jnp.exp(lg_b * jnp.maximum(-rel, 0.0))) * scale
    tab[T_QF] = jnp.exp(lg_f * (ri + 1.0)) * scale
    tab[T_QB] = jnp.exp(lg_b * (ch - ri)) * scale
    tab[T_KF] = jnp.exp(lg_f * (ch - 1.0 - ri))
    tab[T_KB] = jnp.exp(lg_b * ri)
    cdec_f = jnp.exp(lg_f * ch)
    cdec_b = jnp.exp(lg_b * ch)

    def chunk_rows(c):
        return pl.ds(pl.multiple_of(c * ch, ch), ch)

    def kv_step(c, carry):
        rows = chunk_rows(c)
        vc = v_ref[rows, :].astype(F32)
        v2 = jnp.concatenate([vc * tab[T_KF], vc * tab[T_KB]], axis=-1).astype(BF16)
        kv_s[c] = lax.dot_general(k_ref[rows, :], v2, (((0,), (0,)), ((), ())),
                                  preferred_element_type=F32)
        return carry

    lax.fori_loop(0, n_ch, kv_step, 0, unroll=8)

    def scan_f(c, state):
        st_s[c, :, :HEAD_DIM] = state.astype(BF16)
        return state * cdec_f + kv_s[c, :, :HEAD_DIM]

    def scan_b(i, state):
        c = n_ch - 1 - i
        st_s[c, :, HEAD_DIM:] = state.astype(BF16)
        return state * cdec_b + kv_s[c, :, HEAD_DIM:]

    lax.fori_loop(0, n_ch, scan_f, jnp.zeros(sq, F32), unroll=4)
    lax.fori_loop(0, n_ch, scan_b, jnp.zeros(sq, F32), unroll=4)

    def out_step(c, carry):
        rows = chunk_rows(c)
        qb = q_ref[rows, :]
        inner = lax.dot_general(qb, k_ref[rows, :], (((1,), (1,)), ((), ())),
                                preferred_element_type=F32) * tab[T_INTRA]
        o = jnp.dot(inner.astype(BF16), v_ref[rows, :], preferred_element_type=F32)
        cross = jnp.dot(qb, st_s[c], preferred_element_type=F32)
        o = o + cross[:, :HEAD_DIM] * tab[T_QF] + cross[:, HEAD_DIM:] * tab[T_QB]
        o = o * lax.rsqrt(jnp.mean(o * o, axis=-1, keepdims=True) + EPS)
        o = o * rw_ref[...]
        g = g_ref[rows, :].astype(F32)
        o = o * (g * (1.0 / (1.0 + jnp.exp(-g))))
        o_ref[rows, :] = o.astype(o_ref.dtype)
        return carry

    lax.fori_loop(0, n_ch, out_step, 0, unroll=8)


def _retention(decay_f, decay_b, qkvg, ret_norm_w):
    _, batch, heads, seq, dh = qkvg.shape

    def spec(t):
        return pl.BlockSpec((None, None, None, seq, dh),
                            lambda b, h, t=t: (t, b, h, 0, 0))

    smem = pl.BlockSpec(memory_space=pltpu.SMEM)
    return pl.pallas_call(
        _ret_kernel,
        grid=(batch, heads),
        in_specs=[smem, smem, spec(0), spec(1), spec(2), spec(3),
                  pl.BlockSpec((None, 1, dh), lambda b, h: (h, 0, 0))],
        out_specs=pl.BlockSpec((None, None, seq, dh), lambda b, h: (b, h, 0, 0)),
        out_shape=jax.ShapeDtypeStruct((batch, heads, seq, dh), BF16),
        scratch_shapes=[pltpu.VMEM((5, RET_CHUNK, dh), F32),
                        pltpu.VMEM((seq // RET_CHUNK, dh, 2 * dh), F32),
                        pltpu.VMEM((seq // RET_CHUNK, dh, 2 * dh), BF16)],
        compiler_params=_params(("parallel", "parallel"), 32),
        name="retention",
    )(decay_f, decay_b, qkvg, qkvg, qkvg, qkvg, ret_norm_w)


def _outproj_kernel(a_ref, r_ref, x_ref, w_ref, nw_ref, h_ref, n_ref):
    mixed = jnp.concatenate([a_ref[h] for h in range(N_HEADS)]
                            + [r_ref[h] for h in range(N_HEADS)], axis=-1)
    hid = x_ref[...] + jnp.dot(mixed, w_ref[...], preferred_element_type=F32)
    h_ref[...] = hid
    n_ref[...] = _rmsnorm_rows(hid, nw_ref[...]).astype(n_ref.dtype)


def _outproj(attn, ret, x2, w_out, norm_w, tm=512):
    batch, heads, seq, dh = attn.shape
    n_tok, d_model = x2.shape
    s_blocks = seq // tm
    head_spec = pl.BlockSpec((None, heads, tm, dh),
                             lambda i: (i // s_blocks, 0, i % s_blocks, 0))
    row_spec = pl.BlockSpec((tm, d_model), lambda i: (i, 0))
    return pl.pallas_call(
        _outproj_kernel,
        grid=(n_tok // tm,),
        in_specs=[head_spec, head_spec, row_spec,
                  pl.BlockSpec(w_out.shape, lambda i: (0, 0)),
                  pl.BlockSpec((1, d_model), lambda i: (0, 0))],
        out_specs=[row_spec, row_spec],
        out_shape=[jax.ShapeDtypeStruct((n_tok, d_model), F32),
                   jax.ShapeDtypeStruct((n_tok, d_model), BF16)],
        compiler_params=_params(("parallel",), 56),
        name="outproj",
    )(attn, ret, x2, w_out, norm_w)


def _ffn_kernel(n_ref, h_ref, wg_ref, wu_ref, wd_ref, fw_ref, o_ref, acc, *, final_norm):
    f = pl.program_id(1)
    n = n_ref[...]
    g = jnp.dot(n, wg_ref[...], preferred_element_type=F32)
    u = jnp.dot(n, wu_ref[...], preferred_element_type=F32)
    a = (g * (1.0 / (1.0 + jnp.exp(-g))) * u).astype(BF16)
    part = jnp.dot(a, wd_ref[...], preferred_element_type=F32)

    @pl.when(f == 0)
    def _():
        acc[...] = h_ref[...] + part

    @pl.when(f > 0)
    def _():
        acc[...] += part

    @pl.when(f == pl.num_programs(1) - 1)
    def _():
        out = acc[...]
        if final_norm:
            out = _rmsnorm_rows(out, fw_ref[...])
        o_ref[...] = out


def _ffn(n2, hid, w_gate, w_up, w_down, final_w, final_norm, tm=512, tf=512):
    n_tok, d_model = hid.shape
    d_ff = w_gate.shape[1]
    row = lambda i, f: (i, 0)
    return pl.pallas_call(
        functools.partial(_ffn_kernel, final_norm=final_norm),
        grid=(n_tok // tm, d_ff // tf),
        in_specs=[pl.BlockSpec((tm, d_model), row),
                  pl.BlockSpec((tm, d_model), row),
                  pl.BlockSpec((d_model, tf), lambda i, f: (0, f)),
                  pl.BlockSpec((d_model, tf), lambda i, f: (0, f)),
                  pl.BlockSpec((tf, d_model), lambda i, f: (f, 0)),
                  pl.BlockSpec((1, d_model), lambda i, f: (0, 0))],
        out_specs=pl.BlockSpec((tm, d_model), row),
        out_shape=jax.ShapeDtypeStruct((n_tok, d_model), F32),
        scratch_shapes=[pltpu.VMEM((tm, d_model), F32)],
        compiler_params=_params(("parallel", "arbitrary"), 48),
        name="ffn",
    )(n2, hid, w_gate, w_up, w_down, final_w)


def kernel(x, norm_mix_w, w_in, ret_decay_fwd, ret_decay_bwd, ret_norm_w, w_out,
           norm_ffn_w, w_gate, w_up, w_down, norm_final_w):
    batch, seq, d_model = x.shape
    depth = w_in.shape[0]
    assert depth >= 1
    d_att = 3 * D_GROUP
    slopes = jnp.exp2(-8.0 * jnp.arange(1, N_HEADS + 1, dtype=F32) / N_HEADS)
    final_w = norm_final_w.reshape(1, d_model)

    hid = x.reshape(batch * seq, d_model)
    for layer in range(depth):
        w_in_l = w_in[layer]
        nw = norm_mix_w[layer].reshape(1, d_model)
        qkv_a = _inproj(hid, nw, w_in_l[:, :d_att].astype(BF16), batch, seq, F32)
        qkvg_r = _inproj(hid, nw, w_in_l[:, d_att:].astype(BF16), batch, seq, BF16)
        attn = _attention(slopes, qkv_a)
        ret = _retention(ret_decay_fwd[layer], ret_decay_bwd[layer], qkvg_r,
                         ret_norm_w[layer].reshape(N_HEADS, 1, HEAD_DIM))
        hid, n2 = _outproj(attn, ret, hid, w_out[layer].astype(BF16),
                           norm_ffn_w[layer].reshape(1, d_model))
        last = layer == depth - 1
        hid = _ffn(n2, hid, w_gate[layer].astype(BF16), w_up[layer].astype(BF16),
                   w_down[layer].astype(BF16), final_w, final_norm=last)
    return hid.reshape(batch, seq, d_model)
```

```python
import functools

import jax
import jax.numpy as jnp
from jax import lax
from jax.experimental import pallas as pl
from jax.experimental.pallas import tpu as pltpu

F32 = jnp.float32
BF16 = jnp.bfloat16

HEAD_DIM = 128
N_HEADS = 8
D_GROUP = N_HEADS * HEAD_DIM
DILATED_PATTERNS = ((128, 1), (512, 4), (2048, 16))
HALF_BAND = 64
RET_CHUNK = 128
EPS = 1e-6
NEG_BIG = -1e30
LOG2_E = 1.4426950408889634
MIB = 1024 * 1024

ATT_TQ = 128
ATT_TK = ATT_TQ + 2 * HALF_BAND


def _params(semantics, vmem_mib):
    return pltpu.CompilerParams(dimension_semantics=semantics,
                                vmem_limit_bytes=vmem_mib * MIB)


def _to_bf16(x, mul=None):
    return (x if mul is None else x * mul).astype(BF16)


def _rmsnorm_rows(x, w):
    return x * lax.rsqrt(jnp.mean(x * x, axis=-1, keepdims=True) + EPS) * w


def _inproj_kernel(x_ref, nw_ref, w_ref, o_ref, n_scr):
    @pl.when(pl.program_id(1) == 0)
    def _():
        n_scr[...] = _rmsnorm_rows(x_ref[...], nw_ref[...]).astype(BF16)

    res = jnp.dot(n_scr[...], w_ref[...], preferred_element_type=F32)
    for h in range(N_HEADS):
        o_ref[h] = res[:, h * HEAD_DIM:(h + 1) * HEAD_DIM].astype(o_ref.dtype)


def _inproj(x2, norm_w, w, batch, seq, out_dtype, tm=1024):
    n_tok, d_model = x2.shape
    groups = w.shape[1] // D_GROUP
    s_blocks = seq // tm
    return pl.pallas_call(
        _inproj_kernel,
        grid=(n_tok // tm, groups),
        in_specs=[
            pl.BlockSpec((tm, d_model), lambda i, j: (i, 0)),
            pl.BlockSpec((1, d_model), lambda i, j: (0, 0)),
            pl.BlockSpec((d_model, D_GROUP), lambda i, j: (0, j)),
        ],
        out_specs=pl.BlockSpec(
            (None, None, N_HEADS, tm, HEAD_DIM),
            lambda i, j: (j, i // s_blocks, 0, i % s_blocks, 0)),
        out_shape=jax.ShapeDtypeStruct(
            (groups, batch, N_HEADS, seq, HEAD_DIM), out_dtype),
        scratch_shapes=[pltpu.VMEM((tm, d_model), BF16)],
        compiler_params=_params(("parallel", "arbitrary"), 48),
        name="inproj",
    )(x2, norm_w, w)


def _attn_kernel(slopes_ref, q_ref, k_ref, v_ref, o_ref,
                 q4f, k4f, v4f, qs, ks, vs,
                 num4, den4, m4, num1, den1, m1, bias_s):
    seq = q_ref.shape[0]
    quarter = seq // 4
    slope = slopes_ref[pl.program_id(1)]
    q_scale = HEAD_DIM ** -0.5 * LOG2_E

    row = lax.broadcasted_iota(jnp.int32, (ATT_TQ, ATT_TK), 0)
    col = lax.broadcasted_iota(jnp.int32, (ATT_TQ, ATT_TK), 1)
    for p, (_, dil) in enumerate(DILATED_PATTERNS):
        for c, off in enumerate((0, HALF_BAND, 2 * HALF_BAND)):
            dist = jnp.abs(row - col + off)
            bias = -slope * (dil * dist).astype(F32) * LOG2_E
            bias_s[p * 3 + c] = jnp.where(dist <= HALF_BAND, bias, NEG_BIG)

    vs[:, HEAD_DIM:] = jnp.ones((seq, HEAD_DIM), BF16)

    for src, dst in ((q_ref, q4f), (k_ref, k4f), (v_ref, v4f)):
        for r in range(4):
            dst[r * quarter:(r + 1) * quarter, :] = src[pl.ds(r, quarter, stride=4), :]

    def run_pattern(p, dil, incoming, emit):
        sub_len = seq // dil
        n_blk = sub_len // ATT_TQ
        shift = n_blk.bit_length() - 1
        assert n_blk == 1 << shift and n_blk >= 2

        def block(idx, carry):
            r = idx >> shift
            blk = idx & (n_blk - 1)
            q0 = blk * ATT_TQ
            k0 = jnp.clip(q0 - HALF_BAND, 0, sub_len - ATT_TK)
            q_rows = pl.ds(pl.multiple_of(r * sub_len + q0, ATT_TQ), ATT_TQ)
            k_rows = pl.ds(pl.multiple_of(r * sub_len + k0, HALF_BAND), ATT_TK)
            case = jnp.where(blk == 0, 0, jnp.where(blk == n_blk - 1, 2, 1))
            s = lax.dot_general(qs[q_rows, :], ks[k_rows, :], (((1,), (1,)), ((), ())),
                                preferred_element_type=F32)
            s = s + bias_s[p * 3 + case]
            m_blk = jnp.max(s, axis=-1, keepdims=True)
            if incoming is None:
                m_new = jnp.broadcast_to(m_blk, (ATT_TQ, HEAD_DIM))
            else:
                num_in, den_in, m_in = incoming
                m_old = m_in[q_rows, :]
                m_new = jnp.maximum(m_old, m_blk)
            e = jnp.exp2(s - jnp.concatenate([m_new, m_new], axis=-1))
            pv = jnp.dot(e.astype(BF16), vs[k_rows, :], preferred_element_type=F32)
            num = pv[:, :HEAD_DIM]
            den = pv[:, HEAD_DIM:]
            if incoming is not None:
                alpha = jnp.exp2(m_old - m_new)
                num = alpha * num_in[q_rows, :] + num
                den = alpha * den_in[q_rows, :] + den
            emit(r, q0, num, den, m_new)
            return carry

        lax.fori_loop(0, dil * n_blk, block, 0, unroll=16)

    sub16 = seq // 16
    for src, dst, mul in ((q4f, qs, q_scale), (k4f, ks, None), (v4f, vs, None)):
        for r16 in range(16):
            r4, b = r16 % 4, r16 // 4
            dst[r16 * sub16:(r16 + 1) * sub16, :HEAD_DIM] = _to_bf16(
                src[pl.ds(r4 * quarter + b, sub16, stride=4), :], mul)

    def emit16(r, q0, num, den, m_new):
        rows = pl.ds((r & 3) * quarter + 4 * q0 + (r >> 2), ATT_TQ, stride=4)
        num4[rows, :] = num
        den4[rows, :] = den
        m4[rows, :] = m_new

    run_pattern(2, 16, None, emit16)

    for src, dst, mul in ((q4f, qs, q_scale), (k4f, ks, None), (v4f, vs, None)):
        dst[:, :HEAD_DIM] = _to_bf16(src[...], mul)

    def emit4(r, q0, num, den, m_new):
        rows = pl.ds(4 * q0 + r, ATT_TQ, stride=4)
        num1[rows, :] = num
        den1[rows, :] = den
        m1[rows, :] = m_new

    run_pattern(1, 4, (num4, den4, m4), emit4)

    for src, dst, mul in ((q_ref, qs, q_scale), (k_ref, ks, None), (v_ref, vs, None)):
        dst[:, :HEAD_DIM] = _to_bf16(src[...], mul)

    def emit1(r, q0, num, den, m_new):
        o_ref[pl.ds(pl.multiple_of(q0, ATT_TQ), ATT_TQ), :] = (num / den).astype(o_ref.dtype)

    run_pattern(0, 1, (num1, den1, m1), emit1)


def _attention(slopes, qkv):
    _, batch, heads, seq, dh = qkv.shape
    assert seq % (16 * ATT_TK) == 0 and dh == HEAD_DIM

    def spec(t):
        return pl.BlockSpec((None, None, None, seq, dh),
                            lambda b, h, t=t: (t, b, h, 0, 0))

    f32_rows = pltpu.VMEM((seq, dh), F32)
    return pl.pallas_call(
        _attn_kernel,
        grid=(batch, heads),
        in_specs=[pl.BlockSpec(memory_space=pltpu.SMEM), spec(0), spec(1), spec(2)],
        out_specs=pl.BlockSpec((None, None, seq, dh), lambda b, h: (b, h, 0, 0)),
        out_shape=jax.ShapeDtypeStruct((batch, heads, seq, dh), BF16),
        scratch_shapes=[
            f32_rows, f32_rows, f32_rows,
            pltpu.VMEM((seq, dh), BF16), pltpu.VMEM((seq, dh), BF16),
            pltpu.VMEM((seq, 2 * dh), BF16),
            f32_rows, f32_rows, f32_rows, f32_rows, f32_rows, f32_rows,
            pltpu.VMEM((3 * len(DILATED_PATTERNS), ATT_TQ, ATT_TK), F32),
        ],
        compiler_params=_params(("parallel", "parallel"), 48),
        name="dilated_attention",
    )(slopes, qkv, qkv, qkv)


def _ret_kernel(df_ref, db_ref, q_ref, k_ref, v_ref, g_ref, rw_ref, o_ref,
                tab, kv_s, st_s):
    seq = q_ref.shape[0]
    ch = RET_CHUNK
    n_ch = seq // ch
    h = pl.program_id(1)
    scale = HEAD_DIM ** -0.5
    sq = (ch, HEAD_DIM)
    T_INTRA, T_QF, T_QB, T_KF, T_KB = range(5)

    lg_f = -jnp.exp(jnp.full((1, HEAD_DIM), df_ref[h], F32))
    lg_b = -jnp.exp(jnp.full((1, HEAD_DIM), db_ref[h], F32))
    ri = lax.broadcasted_iota(jnp.int32, sq, 0).astype(F32)
    ci = lax.broadcasted_iota(jnp.int32, sq, 1).astype(F32)
    rel = ri - ci
    tab[T_INTRA] = jnp.where(rel >= 0.0,
                             jnp.exp(lg_f * jnp.maximum(rel, 0.0)),
                             jnp.exp(lg_b * jnp.maximum(-rel, 0.0))) * scale
    tab[T_QF] = jnp.exp(lg_f * (ri + 1.0)) * scale
    tab[T_QB] = jnp.exp(lg_b * (ch - ri)) * scale
    tab[T_KF] = jnp.exp(lg_f * (ch - 1.0 - ri))
    tab[T_KB] = jnp.exp(lg_b * ri)
    cdec_f = jnp.exp(lg_f * ch)
    cdec_b = jnp.exp(lg_b * ch)

    def chunk_rows(c):
        return pl.ds(pl.multiple_of(c * ch, ch), ch)

    def kv_step(c, carry):
        rows = chunk_rows(c)
        vc = v_ref[rows, :].astype(F32)
        v2 = jnp.concatenate([vc * tab[T_KF], vc * tab[T_KB]], axis=-1).astype(BF16)
        kv_s[c] = lax.dot_general(k_ref[rows, :], v2, (((0,), (0,)), ((), ())),
                                  preferred_element_type=F32)
        return carry

    lax.fori_loop(0, n_ch, kv_step, 0, unroll=8)

    def scan_f(c, state):
        st_s[c, :, :HEAD_DIM] = state.astype(BF16)
        return state * cdec_f + kv_s[c, :, :HEAD_DIM]

    def scan_b(i, state):
        c = n_ch - 1 - i
        st_s[c, :, HEAD_DIM:] = state.astype(BF16)
        return state * cdec_b + kv_s[c, :, HEAD_DIM:]

    lax.fori_loop(0, n_ch, scan_f, jnp.zeros(sq, F32), unroll=4)
    lax.fori_loop(0, n_ch, scan_b, jnp.zeros(sq, F32), unroll=4)

    def out_step(c, carry):
        rows = chunk_rows(c)
        qb = q_ref[rows, :]
        inner = lax.dot_general(qb, k_ref[rows, :], (((1,), (1,)), ((), ())),
                                preferred_element_type=F32) * tab[T_INTRA]
        o = jnp.dot(inner.astype(BF16), v_ref[rows, :], preferred_element_type=F32)
        cross = jnp.dot(qb, st_s[c], preferred_element_type=F32)
        o = o + cross[:, :HEAD_DIM] * tab[T_QF] + cross[:, HEAD_DIM:] * tab[T_QB]
        o = o * lax.rsqrt(jnp.mean(o * o, axis=-1, keepdims=True) + EPS)
        o = o * rw_ref[...]
        g = g_ref[rows, :].astype(F32)
        o = o * (g * (1.0 / (1.0 + jnp.exp(-g))))
        o_ref[rows, :] = o.astype(o_ref.dtype)
        return carry

    lax.fori_loop(0, n_ch, out_step, 0, unroll=16)


def _retention(decay_f, decay_b, qkvg, ret_norm_w):
    _, batch, heads, seq, dh = qkvg.shape

    def spec(t):
        return pl.BlockSpec((None, None, None, seq, dh),
                            lambda b, h, t=t: (t, b, h, 0, 0))

    smem = pl.BlockSpec(memory_space=pltpu.SMEM)
    return pl.pallas_call(
        _ret_kernel,
        grid=(batch, heads),
        in_specs=[smem, smem, spec(0), spec(1), spec(2), spec(3),
                  pl.BlockSpec((None, 1, dh), lambda b, h: (h, 0, 0))],
        out_specs=pl.BlockSpec((None, None, seq, dh), lambda b, h: (b, h, 0, 0)),
        out_shape=jax.ShapeDtypeStruct((batch, heads, seq, dh), BF16),
        scratch_shapes=[pltpu.VMEM((5, RET_CHUNK, dh), F32),
                        pltpu.VMEM((seq // RET_CHUNK, dh, 2 * dh), F32),
                        pltpu.VMEM((seq // RET_CHUNK, dh, 2 * dh), BF16)],
        compiler_params=_params(("parallel", "parallel"), 32),
        name="retention",
    )(decay_f, decay_b, qkvg, qkvg, qkvg, qkvg, ret_norm_w)


def _outproj_kernel(a_ref, r_ref, x_ref, w_ref, nw_ref, h_ref, n_ref):
    mixed = jnp.concatenate([a_ref[h] for h in range(N_HEADS)]
                            + [r_ref[h] for h in range(N_HEADS)], axis=-1)
    hid = x_ref[...] + jnp.dot(mixed, w_ref[...], preferred_element_type=F32)
    h_ref[...] = hid
    n_ref[...] = _rmsnorm_rows(hid, nw_ref[...]).astype(n_ref.dtype)


def _outproj(attn, ret, x2, w_out, norm_w, tm=512):
    batch, heads, seq, dh = attn.shape
    n_tok, d_model = x2.shape
    s_blocks = seq // tm
    head_spec = pl.BlockSpec((None, heads, tm, dh),
                             lambda i: (i // s_blocks, 0, i % s_blocks, 0))
    row_spec = pl.BlockSpec((tm, d_model), lambda i: (i, 0))
    return pl.pallas_call(
        _outproj_kernel,
        grid=(n_tok // tm,),
        in_specs=[head_spec, head_spec, row_spec,
                  pl.BlockSpec(w_out.shape, lambda i: (0, 0)),
                  pl.BlockSpec((1, d_model), lambda i: (0, 0))],
        out_specs=[row_spec, row_spec],
        out_shape=[jax.ShapeDtypeStruct((n_tok, d_model), F32),
                   jax.ShapeDtypeStruct((n_tok, d_model), BF16)],
        compiler_params=_params(("parallel",), 56),
        name="outproj",
    )(attn, ret, x2, w_out, norm_w)


def _ffn_kernel(n_ref, h_ref, wg_ref, wu_ref, wd_ref, fw_ref, o_ref, acc, *, final_norm):
    f = pl.program_id(1)

    @pl.when(f == 0)
    def _():
        acc[...] = h_ref[...]

    n = n_ref[...]
    g = jnp.dot(n, wg_ref[...], preferred_element_type=F32)
    u = jnp.dot(n, wu_ref[...], preferred_element_type=F32)
    a = (g * (1.0 / (1.0 + jnp.exp(-g))) * u).astype(BF16)
    acc[...] += jnp.dot(a, wd_ref[...], preferred_element_type=F32)

    @pl.when(f == pl.num_programs(1) - 1)
    def _():
        out = acc[...]
        if final_norm:
            out = _rmsnorm_rows(out, fw_ref[...])
        o_ref[...] = out


def _ffn(n2, hid, w_gate, w_up, w_down, final_w, final_norm, tm=512, tf=512):
    n_tok, d_model = hid.shape
    d_ff = w_gate.shape[1]
    row = lambda i, f: (i, 0)
    return pl.pallas_call(
        functools.partial(_ffn_kernel, final_norm=final_norm),
        grid=(n_tok // tm, d_ff // tf),
        in_specs=[pl.BlockSpec((tm, d_model), row),
                  pl.BlockSpec((tm, d_model), row),
                  pl.BlockSpec((d_model, tf), lambda i, f: (0, f)),
                  pl.BlockSpec((d_model, tf), lambda i, f: (0, f)),
                  pl.BlockSpec((tf, d_model), lambda i, f: (f, 0)),
                  pl.BlockSpec((1, d_model), lambda i, f: (0, 0))],
        out_specs=pl.BlockSpec((tm, d_model), row),
        out_shape=jax.ShapeDtypeStruct((n_tok, d_model), F32),
        scratch_shapes=[pltpu.VMEM((tm, d_model), F32)],
        compiler_params=_params(("parallel", "arbitrary"), 48),
        name="ffn",
    )(n2, hid, w_gate, w_up, w_down, final_w)


def kernel(x, norm_mix_w, w_in, ret_decay_fwd, ret_decay_bwd, ret_norm_w, w_out,
           norm_ffn_w, w_gate, w_up, w_down, norm_final_w):
    batch, seq, d_model = x.shape
    depth = w_in.shape[0]
    assert depth >= 1
    d_att = 3 * D_GROUP
    slopes = jnp.exp2(-8.0 * jnp.arange(1, N_HEADS + 1, dtype=F32) / N_HEADS)
    final_w = norm_final_w.reshape(1, d_model)

    hid = x.reshape(batch * seq, d_model)
    for layer in range(depth):
        w_in_l = w_in[layer]
        nw = norm_mix_w[layer].reshape(1, d_model)
        qkv_a = _inproj(hid, nw, w_in_l[:, :d_att].astype(BF16), batch, seq, F32)
        qkvg_r = _inproj(hid, nw, w_in_l[:, d_att:].astype(BF16), batch, seq, BF16)
        attn = _attention(slopes, qkv_a)
        ret = _retention(ret_decay_fwd[layer], ret_decay_bwd[layer], qkvg_r,
                         ret_norm_w[layer].reshape(N_HEADS, 1, HEAD_DIM))
        hid, n2 = _outproj(attn, ret, hid, w_out[layer].astype(BF16),
                           norm_ffn_w[layer].reshape(1, d_model))
        last = layer == depth - 1
        hid = _ffn(n2, hid, w_gate[layer].astype(BF16), w_up[layer].astype(BF16),
                   w_down[layer].astype(BF16), final_w, final_norm=last)
    return hid.reshape(batch, seq, d_model)
```

```python
import functools

import jax
import jax.numpy as jnp
from jax import lax
from jax.experimental import pallas as pl
from jax.experimental.pallas import tpu as pltpu

F32 = jnp.float32
BF16 = jnp.bfloat16

HEAD_DIM = 128
N_HEADS = 8
D_GROUP = N_HEADS * HEAD_DIM
DILATED_PATTERNS = ((128, 1), (512, 4), (2048, 16))
HALF_BAND = 64
RET_CHUNK = 128
EPS = 1e-6
NEG_BIG = -1e30
LOG2_E = 1.4426950408889634
MIB = 1024 * 1024

FFN_TF = 512
ATT_TQ = 128
ATT_TK = ATT_TQ + 2 * HALF_BAND


def _params(semantics, vmem_mib, flags=None):
    return pltpu.CompilerParams(dimension_semantics=semantics,
                                vmem_limit_bytes=vmem_mib * MIB, flags=flags)


def _to_bf16(x, mul=None):
    return (x if mul is None else x * mul).astype(BF16)


def _rmsnorm_rows(x, w):
    return x * lax.rsqrt(jnp.mean(x * x, axis=-1, keepdims=True) + EPS) * w


def _inproj_kernel(x_ref, nw_ref, w_ref, o_ref, n_scr):
    @pl.when(pl.program_id(1) == 0)
    def _():
        n_scr[...] = _rmsnorm_rows(x_ref[...], nw_ref[...]).astype(BF16)

    res = jnp.dot(n_scr[...], w_ref[...], preferred_element_type=F32)
    for h in range(N_HEADS):
        o_ref[h] = res[:, h * HEAD_DIM:(h + 1) * HEAD_DIM].astype(o_ref.dtype)


def _inproj(x2, norm_w, w_t, first_group, groups, batch, seq, out_dtype, tm=1024):
    n_tok, d_model = x2.shape
    s_blocks = seq // tm
    return pl.pallas_call(
        _inproj_kernel,
        grid=(n_tok // tm, groups),
        in_specs=[
            pl.BlockSpec((tm, d_model), lambda i, j: (i, 0)),
            pl.BlockSpec((1, d_model), lambda i, j: (0, 0)),
            pl.BlockSpec((None, d_model, D_GROUP), lambda i, j: (first_group + j, 0, 0)),
        ],
        out_specs=pl.BlockSpec(
            (None, None, N_HEADS, tm, HEAD_DIM),
            lambda i, j: (j, i // s_blocks, 0, i % s_blocks, 0)),
        out_shape=jax.ShapeDtypeStruct(
            (groups, batch, N_HEADS, seq, HEAD_DIM), out_dtype),
        scratch_shapes=[pltpu.VMEM((tm, d_model), BF16)],
        compiler_params=_params(("parallel", "arbitrary"), 48),
        name="inproj",
    )(x2, norm_w, w_t)


def _attn_kernel(slopes_ref, q_ref, k_ref, v_ref, o_ref,
                 q4f, k4f, v4f, qs, ks, vs,
                 num4, den4, m4, num1, den1, m1, bias_s):
    seq = q_ref.shape[0]
    quarter = seq // 4
    q_scale = HEAD_DIM ** -0.5 * LOG2_E

    @pl.when(pl.program_id(1) == 0)
    def _():
        slope = slopes_ref[pl.program_id(0)]
        row = lax.broadcasted_iota(jnp.int32, (ATT_TQ, ATT_TK), 0)
        col = lax.broadcasted_iota(jnp.int32, (ATT_TQ, ATT_TK), 1)
        for p, (_, dil) in enumerate(DILATED_PATTERNS):
            for c, off in enumerate((0, HALF_BAND, 2 * HALF_BAND)):
                dist = jnp.abs(row - col + off)
                bias = -slope * (dil * dist).astype(F32) * LOG2_E
                bias_s[p * 3 + c] = jnp.where(dist <= HALF_BAND, bias, NEG_BIG)
        vs[:, HEAD_DIM:] = jnp.ones((seq, HEAD_DIM), BF16)

    for src, dst in ((q_ref, q4f), (k_ref, k4f), (v_ref, v4f)):
        for r in range(4):
            dst[r * quarter:(r + 1) * quarter, :] = src[pl.ds(r, quarter, stride=4), :]

    def run_pattern(p, dil, incoming, emit):
        sub_len = seq // dil
        n_blk = sub_len // ATT_TQ
        shift = n_blk.bit_length() - 1
        assert n_blk == 1 << shift and n_blk >= 2

        def block(idx, carry):
            r = idx >> shift
            blk = idx & (n_blk - 1)
            q0 = blk * ATT_TQ
            k0 = jnp.clip(q0 - HALF_BAND, 0, sub_len - ATT_TK)
            q_rows = pl.ds(pl.multiple_of(r * sub_len + q0, ATT_TQ), ATT_TQ)
            k_rows = pl.ds(pl.multiple_of(r * sub_len + k0, HALF_BAND), ATT_TK)
            case = jnp.where(blk == 0, 0, jnp.where(blk == n_blk - 1, 2, 1))
            s = lax.dot_general(qs[q_rows, :], ks[k_rows, :], (((1,), (1,)), ((), ())),
                                preferred_element_type=F32)
            s = s + bias_s[p * 3 + case]
            m_blk = jnp.max(s, axis=-1, keepdims=True)
            if incoming is None:
                m_new = jnp.broadcast_to(m_blk, (ATT_TQ, HEAD_DIM))
            else:
                num_in, den_in, m_in = incoming
                m_old = m_in[q_rows, :]
                m_new = jnp.maximum(m_old, m_blk)
            e = jnp.exp2(s - jnp.concatenate([m_new, m_new], axis=-1))
            pv = jnp.dot(e.astype(BF16), vs[k_rows, :], preferred_element_type=F32)
            num = pv[:, :HEAD_DIM]
            den = pv[:, HEAD_DIM:]
            if incoming is not None:
                alpha = jnp.exp2(m_old - m_new)
                num = alpha * num_in[q_rows, :] + num
                den = alpha * den_in[q_rows, :] + den
            emit(r, q0, num, den, m_new)
            return carry

        lax.fori_loop(0, dil * n_blk, block, 0, unroll=16)

    sub16 = seq // 16
    for src, dst, mul in ((q4f, qs, q_scale), (k4f, ks, None), (v4f, vs, None)):
        for r16 in range(16):
            r4, b = r16 % 4, r16 // 4
            dst[r16 * sub16:(r16 + 1) * sub16, :HEAD_DIM] = _to_bf16(
                src[pl.ds(r4 * quarter + b, sub16, stride=4), :], mul)

    def emit16(r, q0, num, den, m_new):
        rows = pl.ds((r & 3) * quarter + 4 * q0 + (r >> 2), ATT_TQ, stride=4)
        num4[rows, :] = num
        den4[rows, :] = den
        m4[rows, :] = m_new

    run_pattern(2, 16, None, emit16)

    for src, dst, mul in ((q4f, qs, q_scale), (k4f, ks, None), (v4f, vs, None)):
        dst[:, :HEAD_DIM] = _to_bf16(src[...], mul)

    def emit4(r, q0, num, den, m_new):
        rows = pl.ds(4 * q0 + r, ATT_TQ, stride=4)
        num1[rows, :] = num
        den1[rows, :] = den
        m1[rows, :] = m_new

    run_pattern(1, 4, (num4, den4, m4), emit4)

    for src, dst, mul in ((q_ref, qs, q_scale), (k_ref, ks, None), (v_ref, vs, None)):
        dst[:, :HEAD_DIM] = _to_bf16(src[...], mul)

    def emit1(r, q0, num, den, m_new):
        o_ref[pl.ds(pl.multiple_of(q0, ATT_TQ), ATT_TQ), :] = (num / den).astype(o_ref.dtype)

    run_pattern(0, 1, (num1, den1, m1), emit1)


def _attention(slopes, qkv):
    _, batch, heads, seq, dh = qkv.shape
    assert seq % (16 * ATT_TK) == 0 and dh == HEAD_DIM

    def spec(t):
        return pl.BlockSpec((None, None, None, seq, dh),
                            lambda h, b, t=t: (t, b, h, 0, 0))

    f32_rows = pltpu.VMEM((seq, dh), F32)
    return pl.pallas_call(
        _attn_kernel,
        grid=(heads, batch),
        in_specs=[pl.BlockSpec(memory_space=pltpu.SMEM), spec(0), spec(1), spec(2)],
        out_specs=pl.BlockSpec((None, None, seq, dh), lambda h, b: (b, h, 0, 0)),
        out_shape=jax.ShapeDtypeStruct((batch, heads, seq, dh), BF16),
        scratch_shapes=[
            f32_rows, f32_rows, f32_rows,
            pltpu.VMEM((seq, dh), BF16), pltpu.VMEM((seq, dh), BF16),
            pltpu.VMEM((seq, 2 * dh), BF16),
            f32_rows, f32_rows, f32_rows, f32_rows, f32_rows, f32_rows,
            pltpu.VMEM((3 * len(DILATED_PATTERNS), ATT_TQ, ATT_TK), F32),
        ],
        compiler_params=_params(("parallel", "arbitrary"), 48),
        name="dilated_attention",
    )(slopes, qkv, qkv, qkv)


def _ret_kernel(df_ref, db_ref, q_ref, k_ref, v_ref, g_ref, rw_ref, o_ref,
                tab, kv_s, st_s):
    seq = q_ref.shape[0]
    ch = RET_CHUNK
    n_ch = seq // ch
    h = pl.program_id(1)
    scale = HEAD_DIM ** -0.5
    sq = (ch, HEAD_DIM)
    T_INTRA, T_QF, T_QB, T_KF, T_KB = range(5)

    lg_f = -jnp.exp(jnp.full((1, HEAD_DIM), df_ref[h], F32))
    lg_b = -jnp.exp(jnp.full((1, HEAD_DIM), db_ref[h], F32))
    ri = lax.broadcasted_iota(jnp.int32, sq, 0).astype(F32)
    ci = lax.broadcasted_iota(jnp.int32, sq, 1).astype(F32)
    rel = ri - ci
    tab[T_INTRA] = jnp.where(rel >= 0.0,
                             jnp.exp(lg_f * jnp.maximum(rel, 0.0)),
                             jnp.exp(lg_b * jnp.maximum(-rel, 0.0))) * scale
    tab[T_QF] = jnp.exp(lg_f * (ri + 1.0)) * scale
    tab[T_QB] = jnp.exp(lg_b * (ch - ri)) * scale
    tab[T_KF] = jnp.exp(lg_f * (ch - 1.0 - ri))
    tab[T_KB] = jnp.exp(lg_b * ri)
    cdec_f = jnp.exp(lg_f * ch)
    cdec_b = jnp.exp(lg_b * ch)

    def chunk_rows(c):
        return pl.ds(pl.multiple_of(c * ch, ch), ch)

    def kv_step(c, carry):
        rows = chunk_rows(c)
        vc = v_ref[rows, :].astype(F32)
        v2 = jnp.concatenate([vc * tab[T_KF], vc * tab[T_KB]], axis=-1).astype(BF16)
        kv_s[c] = lax.dot_general(k_ref[rows, :], v2, (((0,), (0,)), ((), ())),
                                  preferred_element_type=F32)
        return carry

    lax.fori_loop(0, n_ch, kv_step, 0, unroll=8)

    def scan_f(c, state):
        st_s[c, :, :HEAD_DIM] = state.astype(BF16)
        return state * cdec_f + kv_s[c, :, :HEAD_DIM]

    def scan_b(i, state):
        c = n_ch - 1 - i
        st_s[c, :, HEAD_DIM:] = state.astype(BF16)
        return state * cdec_b + kv_s[c, :, HEAD_DIM:]

    lax.fori_loop(0, n_ch, scan_f, jnp.zeros(sq, F32), unroll=4)
    lax.fori_loop(0, n_ch, scan_b, jnp.zeros(sq, F32), unroll=4)

    def out_step(c, carry):
        rows = chunk_rows(c)
        qb = q_ref[rows, :]
        inner = lax.dot_general(qb, k_ref[rows, :], (((1,), (1,)), ((), ())),
                                preferred_element_type=F32) * tab[T_INTRA]
        o = jnp.dot(inner.astype(BF16), v_ref[rows, :], preferred_element_type=F32)
        cross = jnp.dot(qb, st_s[c], preferred_element_type=F32)
        o = o + cross[:, :HEAD_DIM] * tab[T_QF] + cross[:, HEAD_DIM:] * tab[T_QB]
        o = o * lax.rsqrt(jnp.mean(o * o, axis=-1, keepdims=True) + EPS)
        o = o * rw_ref[...]
        g = g_ref[rows, :].astype(F32)
        o = o * (g * (1.0 / (1.0 + jnp.exp(-g))))
        o_ref[rows, :] = o.astype(o_ref.dtype)
        return carry

    lax.fori_loop(0, n_ch, out_step, 0, unroll=32)


def _retention(decay_f, decay_b, qkvg, ret_norm_w):
    _, batch, heads, seq, dh = qkvg.shape

    def spec(t):
        return pl.BlockSpec((None, None, None, seq, dh),
                            lambda b, h, t=t: (t, b, h, 0, 0))

    smem = pl.BlockSpec(memory_space=pltpu.SMEM)
    return pl.pallas_call(
        _ret_kernel,
        grid=(batch, heads),
        in_specs=[smem, smem, spec(0), spec(1), spec(2), spec(3),
                  pl.BlockSpec((None, 1, dh), lambda b, h: (h, 0, 0))],
        out_specs=pl.BlockSpec((None, None, seq, dh), lambda b, h: (b, h, 0, 0)),
        out_shape=jax.ShapeDtypeStruct((batch, heads, seq, dh), BF16),
        scratch_shapes=[pltpu.VMEM((5, RET_CHUNK, dh), F32),
                        pltpu.VMEM((seq // RET_CHUNK, dh, 2 * dh), F32),
                        pltpu.VMEM((seq // RET_CHUNK, dh, 2 * dh), BF16)],
        compiler_params=_params(("parallel", "parallel"), 32),
        name="retention",
    )(decay_f, decay_b, qkvg, qkvg, qkvg, qkvg, ret_norm_w)


def _outproj_kernel(a_ref, r_ref, x_ref, w_ref, nw_ref, h_ref, n_ref):
    mixed = jnp.concatenate([a_ref[h] for h in range(N_HEADS)]
                            + [r_ref[h] for h in range(N_HEADS)], axis=-1)
    hid = x_ref[...] + jnp.dot(mixed, w_ref[...], preferred_element_type=F32)
    h_ref[...] = hid
    n_ref[...] = _rmsnorm_rows(hid, nw_ref[...]).astype(n_ref.dtype)


def _outproj(attn, ret, x2, w_out, norm_w, tm=512):
    batch, heads, seq, dh = attn.shape
    n_tok, d_model = x2.shape
    s_blocks = seq // tm
    head_spec = pl.BlockSpec((None, heads, tm, dh),
                             lambda i: (i // s_blocks, 0, i % s_blocks, 0))
    row_spec = pl.BlockSpec((tm, d_model), lambda i: (i, 0))
    return pl.pallas_call(
        _outproj_kernel,
        grid=(n_tok // tm,),
        in_specs=[head_spec, head_spec, row_spec,
                  pl.BlockSpec(w_out.shape, lambda i: (0, 0)),
                  pl.BlockSpec((1, d_model), lambda i: (0, 0))],
        out_specs=[row_spec, row_spec],
        out_shape=[jax.ShapeDtypeStruct((n_tok, d_model), F32),
                   jax.ShapeDtypeStruct((n_tok, d_model), BF16)],
        compiler_params=_params(("parallel",), 56),
        name="outproj",
    )(attn, ret, x2, w_out, norm_w)


def _ffn_kernel(n_ref, h_ref, wg_ref, wu_ref, wd_ref, fw_ref, o_ref, *,
                final_norm, h_slices):
    f = pl.program_id(1)
    h_rows = h_ref.shape[0]

    @pl.when(f == 0)
    def _():
        o_ref[...] = jnp.zeros_like(o_ref)

    n = n_ref[...]
    g = jnp.dot(n, wg_ref[...], preferred_element_type=F32)
    u = jnp.dot(n, wu_ref[...], preferred_element_type=F32)
    a = (g * (1.0 / (1.0 + jnp.exp(-g))) * u).astype(BF16)
    o_ref[...] += jnp.dot(a, wd_ref[...], preferred_element_type=F32)

    slice_idx = jnp.minimum(f, h_slices - 1)
    rows = pl.ds(pl.multiple_of(slice_idx * h_rows, h_rows), h_rows)
    o_ref[rows, :] += h_ref[...] * (f < h_slices).astype(F32)

    if final_norm:
        @pl.when(f == pl.num_programs(1) - 1)
        def _():
            o_ref[...] = _rmsnorm_rows(o_ref[...], fw_ref[...])


def _ffn(n2, hid, w_gate_t, w_up_t, w_down, final_w, final_norm, tm=1024, h_slices=8):
    n_tok, d_model = hid.shape
    f_steps, _, tf = w_gate_t.shape
    assert f_steps >= h_slices and tm % h_slices == 0
    row = lambda i, f: (i, 0)
    col_tile = pl.BlockSpec((None, d_model, tf), lambda i, f: (f, 0, 0))
    return pl.pallas_call(
        functools.partial(_ffn_kernel, final_norm=final_norm, h_slices=h_slices),
        grid=(n_tok // tm, f_steps),
        in_specs=[pl.BlockSpec((tm, d_model), row),
                  pl.BlockSpec((tm // h_slices, d_model),
                               lambda i, f: (i * h_slices + jnp.minimum(f, h_slices - 1), 0)),
                  col_tile, col_tile,
                  pl.BlockSpec((tf, d_model), lambda i, f: (f, 0)),
                  pl.BlockSpec((1, d_model), lambda i, f: (0, 0))],
        out_specs=pl.BlockSpec((tm, d_model), row),
        out_shape=jax.ShapeDtypeStruct((n_tok, d_model), F32),
        compiler_params=_params(("parallel", "arbitrary"), 56),
        name="ffn",
    )(n2, hid, w_gate_t, w_up_t, w_down, final_w)


def _col_tiles(w, tile):
    k, n = w.shape
    return w.astype(BF16).reshape(k, n // tile, tile).transpose(1, 0, 2)


def kernel(x, norm_mix_w, w_in, ret_decay_fwd, ret_decay_bwd, ret_norm_w, w_out,
           norm_ffn_w, w_gate, w_up, w_down, norm_final_w):
    batch, seq, d_model = x.shape
    depth = w_in.shape[0]
    assert depth >= 1
    slopes = jnp.exp2(-8.0 * jnp.arange(1, N_HEADS + 1, dtype=F32) / N_HEADS)
    final_w = norm_final_w.reshape(1, d_model)

    hid = x.reshape(batch * seq, d_model)
    for layer in range(depth):
        w_in_l = _col_tiles(w_in[layer], D_GROUP)
        nw = norm_mix_w[layer].reshape(1, d_model)
        qkv_a = _inproj(hid, nw, w_in_l, 0, 3, batch, seq, F32)
        qkvg_r = _inproj(hid, nw, w_in_l, 3, 4, batch, seq, BF16)
        attn = _attention(slopes, qkv_a)
        ret = _retention(ret_decay_fwd[layer], ret_decay_bwd[layer], qkvg_r,
                         ret_norm_w[layer].reshape(N_HEADS, 1, HEAD_DIM))
        hid, n2 = _outproj(attn, ret, hid, w_out[layer].astype(BF16),
                           norm_ffn_w[layer].reshape(1, d_model))
        last = layer == depth - 1
        hid = _ffn(n2, hid, _col_tiles(w_gate[layer], FFN_TF), _col_tiles(w_up[layer], FFN_TF),
                   w_down[layer].astype(BF16), final_w, final_norm=last)
    return hid.reshape(batch, seq, d_model)
```

```python
import functools

import jax
import jax.numpy as jnp
from jax import lax
from jax.experimental import pallas as pl
from jax.experimental.pallas import tpu as pltpu

F32 = jnp.float32
BF16 = jnp.bfloat16

HEAD_DIM = 128
N_HEADS = 8
D_GROUP = N_HEADS * HEAD_DIM
DILATED_PATTERNS = ((128, 1), (512, 4), (2048, 16))
HALF_BAND = 64
RET_CHUNK = 128
EPS = 1e-6
NEG_BIG = -1e30
LOG2_E = 1.4426950408889634
MIB = 1024 * 1024

FFN_TF = 512
ATT_TQ = 128
ATT_TK = ATT_TQ + 2 * HALF_BAND


def _params(semantics, vmem_mib, flags=None):
    return pltpu.CompilerParams(dimension_semantics=semantics,
                                vmem_limit_bytes=vmem_mib * MIB, flags=flags)


def _to_bf16(x, mul=None):
    return (x if mul is None else x * mul).astype(BF16)


def _rmsnorm_rows(x, w):
    return x * lax.rsqrt(jnp.mean(x * x, axis=-1, keepdims=True) + EPS) * w


def _inproj_kernel(x_ref, nw_ref, w_ref, oa_ref, ob_ref, n_scr, *, groups_a):
    j = pl.program_id(1)

    @pl.when(j == 0)
    def _():
        n_scr[...] = _rmsnorm_rows(x_ref[...], nw_ref[...]).astype(BF16)
        ob_ref[...] = jnp.zeros_like(ob_ref)

    def project(o_ref):
        res = jnp.dot(n_scr[...], w_ref[...], preferred_element_type=F32)
        for h in range(N_HEADS):
            o_ref[h] = res[:, h * HEAD_DIM:(h + 1) * HEAD_DIM].astype(o_ref.dtype)

    @pl.when(j < groups_a)
    def _():
        project(oa_ref)

    @pl.when(j >= groups_a)
    def _():
        project(ob_ref)


def _inproj(x2, norm_w, w, groups_a, dtype_a, dtype_b, batch, seq, tm=1024):
    n_tok, d_model = x2.shape
    groups = w.shape[1] // D_GROUP
    groups_b = groups - groups_a
    s_blocks = seq // tm

    def out_spec(group_of_j):
        return pl.BlockSpec(
            (None, None, N_HEADS, tm, HEAD_DIM),
            lambda i, j: (group_of_j(j), i // s_blocks, 0, i % s_blocks, 0))

    def out_shape(n_groups, dtype):
        return jax.ShapeDtypeStruct((n_groups, batch, N_HEADS, seq, HEAD_DIM), dtype)

    return pl.pallas_call(
        functools.partial(_inproj_kernel, groups_a=groups_a),
        grid=(n_tok // tm, groups),
        in_specs=[
            pl.BlockSpec((tm, d_model), lambda i, j: (i, 0)),
            pl.BlockSpec((1, d_model), lambda i, j: (0, 0)),
            pl.BlockSpec((d_model, D_GROUP), lambda i, j: (0, j)),
        ],
        out_specs=[out_spec(lambda j: jnp.minimum(j, groups_a - 1)),
                   out_spec(lambda j: jnp.maximum(j - groups_a, 0))],
        out_shape=[out_shape(groups_a, dtype_a), out_shape(groups_b, dtype_b)],
        scratch_shapes=[pltpu.VMEM((tm, d_model), BF16)],
        compiler_params=_params(("parallel", "arbitrary"), 56),
        name="inproj",
    )(x2, norm_w, w)


def _attn_kernel(slopes_ref, q_ref, k_ref, v_ref, o_ref,
                 q4f, k4f, v4f, qs, ks, vs,
                 num4, den4, m4, num1, den1, m1, bias_s):
    seq = q_ref.shape[0]
    quarter = seq // 4
    q_scale = HEAD_DIM ** -0.5 * LOG2_E

    @pl.when(pl.program_id(1) == 0)
    def _():
        slope = slopes_ref[pl.program_id(0)]
        row = lax.broadcasted_iota(jnp.int32, (ATT_TQ, ATT_TK), 0)
        col = lax.broadcasted_iota(jnp.int32, (ATT_TQ, ATT_TK), 1)
        for p, (_, dil) in enumerate(DILATED_PATTERNS):
            for c, off in enumerate((0, HALF_BAND, 2 * HALF_BAND)):
                dist = jnp.abs(row - col + off)
                bias = -slope * (dil * dist).astype(F32) * LOG2_E
                bias_s[p * 3 + c] = jnp.where(dist <= HALF_BAND, bias, NEG_BIG)
        vs[:, HEAD_DIM:] = jnp.ones((seq, HEAD_DIM), BF16)

    for src, dst in ((q_ref, q4f), (k_ref, k4f), (v_ref, v4f)):
        for r in range(4):
            dst[r * quarter:(r + 1) * quarter, :] = src[pl.ds(r, quarter, stride=4), :]

    def run_pattern(p, dil, incoming, emit):
        sub_len = seq // dil
        n_blk = sub_len // ATT_TQ
        shift = n_blk.bit_length() - 1
        assert n_blk == 1 << shift and n_blk >= 2

        def block(idx, carry):
            r = idx >> shift
            blk = idx & (n_blk - 1)
            q0 = blk * ATT_TQ
            k0 = jnp.clip(q0 - HALF_BAND, 0, sub_len - ATT_TK)
            q_rows = pl.ds(pl.multiple_of(r * sub_len + q0, ATT_TQ), ATT_TQ)
            k_rows = pl.ds(pl.multiple_of(r * sub_len + k0, HALF_BAND), ATT_TK)
            case = jnp.where(blk == 0, 0, jnp.where(blk == n_blk - 1, 2, 1))
            s = lax.dot_general(qs[q_rows, :], ks[k_rows, :], (((1,), (1,)), ((), ())),
                                preferred_element_type=F32)
            s = s + bias_s[p * 3 + case]
            m_blk = jnp.max(s, axis=-1, keepdims=True)
            if incoming is None:
                m_new = jnp.broadcast_to(m_blk, (ATT_TQ, HEAD_DIM))
            else:
                num_in, den_in, m_in = incoming
                m_old = m_in[q_rows, :]
                m_new = jnp.maximum(m_old, m_blk)
            e = jnp.exp2(s - jnp.concatenate([m_new, m_new], axis=-1))
            pv = jnp.dot(e.astype(BF16), vs[k_rows, :], preferred_element_type=F32)
            num = pv[:, :HEAD_DIM]
            den = pv[:, HEAD_DIM:]
            if incoming is not None:
                alpha = jnp.exp2(m_old - m_new)
                num = alpha * num_in[q_rows, :] + num
                den = alpha * den_in[q_rows, :] + den
            emit(r, q0, num, den, m_new)
            return carry

        lax.fori_loop(0, dil * n_blk, block, 0, unroll=16)

    sub16 = seq // 16
    for src, dst, mul in ((q4f, qs, q_scale), (k4f, ks, None), (v4f, vs, None)):
        for r16 in range(16):
            r4, b = r16 % 4, r16 // 4
            dst[r16 * sub16:(r16 + 1) * sub16, :HEAD_DIM] = _to_bf16(
                src[pl.ds(r4 * quarter + b, sub16, stride=4), :], mul)

    def emit16(r, q0, num, den, m_new):
        rows = pl.ds((r & 3) * quarter + 4 * q0 + (r >> 2), ATT_TQ, stride=4)
        num4[rows, :] = num
        den4[rows, :] = den
        m4[rows, :] = m_new

    run_pattern(2, 16, None, emit16)

    for src, dst, mul in ((q4f, qs, q_scale), (k4f, ks, None), (v4f, vs, None)):
        dst[:, :HEAD_DIM] = _to_bf16(src[...], mul)

    def emit4(r, q0, num, den, m_new):
        rows = pl.ds(4 * q0 + r, ATT_TQ, stride=4)
        num1[rows, :] = num
        den1[rows, :] = den
        m1[rows, :] = m_new

    run_pattern(1, 4, (num4, den4, m4), emit4)

    for src, dst, mul in ((q_ref, qs, q_scale), (k_ref, ks, None), (v_ref, vs, None)):
        dst[:, :HEAD_DIM] = _to_bf16(src[...], mul)

    def emit1(r, q0, num, den, m_new):
        o_ref[pl.ds(pl.multiple_of(q0, ATT_TQ), ATT_TQ), :] = (num / den).astype(o_ref.dtype)

    run_pattern(0, 1, (num1, den1, m1), emit1)


def _attention(slopes, qkv):
    _, batch, heads, seq, dh = qkv.shape
    assert seq % (16 * ATT_TK) == 0 and dh == HEAD_DIM

    def spec(t):
        return pl.BlockSpec((None, None, None, seq, dh),
                            lambda h, b, t=t: (t, b, h, 0, 0))

    f32_rows = pltpu.VMEM((seq, dh), F32)
    return pl.pallas_call(
        _attn_kernel,
        grid=(heads, batch),
        in_specs=[pl.BlockSpec(memory_space=pltpu.SMEM), spec(0), spec(1), spec(2)],
        out_specs=pl.BlockSpec((None, None, seq, dh), lambda h, b: (b, h, 0, 0)),
        out_shape=jax.ShapeDtypeStruct((batch, heads, seq, dh), BF16),
        scratch_shapes=[
            f32_rows, f32_rows, f32_rows,
            pltpu.VMEM((seq, dh), BF16), pltpu.VMEM((seq, dh), BF16),
            pltpu.VMEM((seq, 2 * dh), BF16),
            f32_rows, f32_rows, f32_rows, f32_rows, f32_rows, f32_rows,
            pltpu.VMEM((3 * len(DILATED_PATTERNS), ATT_TQ, ATT_TK), F32),
        ],
        compiler_params=_params(("parallel", "arbitrary"), 48),
        name="dilated_attention",
    )(slopes, qkv, qkv, qkv)


def _ret_kernel(df_ref, db_ref, q_ref, k_ref, v_ref, g_ref, rw_ref, o_ref,
                tab, kv_s, st_s):
    seq = q_ref.shape[0]
    ch = RET_CHUNK
    n_ch = seq // ch
    h = pl.program_id(1)
    scale = HEAD_DIM ** -0.5
    sq = (ch, HEAD_DIM)
    T_INTRA, T_QF, T_QB, T_KF, T_KB = range(5)

    lg_f = -jnp.exp(jnp.full((1, HEAD_DIM), df_ref[h], F32))
    lg_b = -jnp.exp(jnp.full((1, HEAD_DIM), db_ref[h], F32))
    ri = lax.broadcasted_iota(jnp.int32, sq, 0).astype(F32)
    ci = lax.broadcasted_iota(jnp.int32, sq, 1).astype(F32)
    rel = ri - ci
    tab[T_INTRA] = jnp.where(rel >= 0.0,
                             jnp.exp(lg_f * jnp.maximum(rel, 0.0)),
                             jnp.exp(lg_b * jnp.maximum(-rel, 0.0))) * scale
    tab[T_QF] = jnp.exp(lg_f * (ri + 1.0)) * scale
    tab[T_QB] = jnp.exp(lg_b * (ch - ri)) * scale
    tab[T_KF] = jnp.exp(lg_f * (ch - 1.0 - ri))
    tab[T_KB] = jnp.exp(lg_b * ri)
    cdec_f = jnp.exp(lg_f * ch)
    cdec_b = jnp.exp(lg_b * ch)

    def chunk_rows(c):
        return pl.ds(pl.multiple_of(c * ch, ch), ch)

    def kv_step(c, carry):
        rows = chunk_rows(c)
        vc = v_ref[rows, :].astype(F32)
        v2 = jnp.concatenate([vc * tab[T_KF], vc * tab[T_KB]], axis=-1).astype(BF16)
        kv_s[c] = lax.dot_general(k_ref[rows, :], v2, (((0,), (0,)), ((), ())),
                                  preferred_element_type=F32)
        return carry

    lax.fori_loop(0, n_ch, kv_step, 0, unroll=8)

    def scan_f(c, state):
        st_s[c, :, :HEAD_DIM] = state.astype(BF16)
        return state * cdec_f + kv_s[c, :, :HEAD_DIM]

    def scan_b(i, state):
        c = n_ch - 1 - i
        st_s[c, :, HEAD_DIM:] = state.astype(BF16)
        return state * cdec_b + kv_s[c, :, HEAD_DIM:]

    lax.fori_loop(0, n_ch, scan_f, jnp.zeros(sq, F32), unroll=4)
    lax.fori_loop(0, n_ch, scan_b, jnp.zeros(sq, F32), unroll=4)

    def out_step(c, carry):
        rows = chunk_rows(c)
        qb = q_ref[rows, :]
        inner = lax.dot_general(qb, k_ref[rows, :], (((1,), (1,)), ((), ())),
                                preferred_element_type=F32) * tab[T_INTRA]
        o = jnp.dot(inner.astype(BF16), v_ref[rows, :], preferred_element_type=F32)
        cross = jnp.dot(qb, st_s[c], preferred_element_type=F32)
        o = o + cross[:, :HEAD_DIM] * tab[T_QF] + cross[:, HEAD_DIM:] * tab[T_QB]
        o = o * lax.rsqrt(jnp.mean(o * o, axis=-1, keepdims=True) + EPS)
        o = o * rw_ref[...]
        g = g_ref[rows, :].astype(F32)
        o = o * (g * (1.0 / (1.0 + jnp.exp(-g))))
        o_ref[rows, :] = o.astype(o_ref.dtype)
        return carry

    lax.fori_loop(0, n_ch, out_step, 0, unroll=32)


def _retention(decay_f, decay_b, qkvg, ret_norm_w):
    _, batch, heads, seq, dh = qkvg.shape

    def spec(t):
        return pl.BlockSpec((None, None, None, seq, dh),
                            lambda b, h, t=t: (t, b, h, 0, 0))

    smem = pl.BlockSpec(memory_space=pltpu.SMEM)
    return pl.pallas_call(
        _ret_kernel,
        grid=(batch, heads),
        in_specs=[smem, smem, spec(0), spec(1), spec(2), spec(3),
                  pl.BlockSpec((None, 1, dh), lambda b, h: (h, 0, 0))],
        out_specs=pl.BlockSpec((None, None, seq, dh), lambda b, h: (b, h, 0, 0)),
        out_shape=jax.ShapeDtypeStruct((batch, heads, seq, dh), BF16),
        scratch_shapes=[pltpu.VMEM((5, RET_CHUNK, dh), F32),
                        pltpu.VMEM((seq // RET_CHUNK, dh, 2 * dh), F32),
                        pltpu.VMEM((seq // RET_CHUNK, dh, 2 * dh), BF16)],
        compiler_params=_params(("parallel", "parallel"), 32),
        name="retention",
    )(decay_f, decay_b, qkvg, qkvg, qkvg, qkvg, ret_norm_w)


def _outproj_kernel(a_ref, r_ref, x_ref, w_ref, nw_ref, h_ref, n_ref):
    mixed = jnp.concatenate([a_ref[h] for h in range(N_HEADS)]
                            + [r_ref[h] for h in range(N_HEADS)], axis=-1)
    hid = x_ref[...] + jnp.dot(mixed, w_ref[...], preferred_element_type=F32)
    h_ref[...] = hid
    n_ref[...] = _rmsnorm_rows(hid, nw_ref[...]).astype(n_ref.dtype)


def _outproj(attn, ret, x2, w_out, norm_w, tm=512):
    batch, heads, seq, dh = attn.shape
    n_tok, d_model = x2.shape
    s_blocks = seq // tm
    head_spec = pl.BlockSpec((None, heads, tm, dh),
                             lambda i: (i // s_blocks, 0, i % s_blocks, 0))
    row_spec = pl.BlockSpec((tm, d_model), lambda i: (i, 0))
    return pl.pallas_call(
        _outproj_kernel,
        grid=(n_tok // tm,),
        in_specs=[head_spec, head_spec, row_spec,
                  pl.BlockSpec(w_out.shape, lambda i: (0, 0)),
                  pl.BlockSpec((1, d_model), lambda i: (0, 0))],
        out_specs=[row_spec, row_spec],
        out_shape=[jax.ShapeDtypeStruct((n_tok, d_model), F32),
                   jax.ShapeDtypeStruct((n_tok, d_model), BF16)],
        compiler_params=_params(("parallel",), 56),
        name="outproj",
    )(attn, ret, x2, w_out, norm_w)


def _ffn_kernel(n_ref, h_ref, wg_ref, wu_ref, wd_ref, fw_ref, o_ref, *,
                final_norm, h_slices):
    f = pl.program_id(1)
    h_rows = h_ref.shape[0]

    @pl.when(f == 0)
    def _():
        o_ref[...] = jnp.zeros_like(o_ref)

    n = n_ref[...]
    g = jnp.dot(n, wg_ref[...], preferred_element_type=F32)
    u = jnp.dot(n, wu_ref[...], preferred_element_type=F32)
    a = (g * (1.0 / (1.0 + jnp.exp(-g))) * u).astype(BF16)
    o_ref[...] += jnp.dot(a, wd_ref[...], preferred_element_type=F32)

    slice_idx = jnp.minimum(f, h_slices - 1)
    rows = pl.ds(pl.multiple_of(slice_idx * h_rows, h_rows), h_rows)
    o_ref[rows, :] += h_ref[...] * (f < h_slices).astype(F32)

    if final_norm:
        @pl.when(f == pl.num_programs(1) - 1)
        def _():
            o_ref[...] = _rmsnorm_rows(o_ref[...], fw_ref[...])


def _ffn(n2, hid, w_gate, w_up, w_down, final_w, final_norm, tm=1024, tf=FFN_TF,
         h_slices=8):
    n_tok, d_model = hid.shape
    f_steps = w_gate.shape[1] // tf
    assert f_steps >= h_slices and tm % h_slices == 0
    row = lambda i, f: (i, 0)
    col_tile = pl.BlockSpec((d_model, tf), lambda i, f: (0, f))
    return pl.pallas_call(
        functools.partial(_ffn_kernel, final_norm=final_norm, h_slices=h_slices),
        grid=(n_tok // tm, f_steps),
        in_specs=[pl.BlockSpec((tm, d_model), row),
                  pl.BlockSpec((tm // h_slices, d_model),
                               lambda i, f: (i * h_slices + jnp.minimum(f, h_slices - 1), 0)),
                  col_tile, col_tile,
                  pl.BlockSpec((tf, d_model), lambda i, f: (f, 0)),
                  pl.BlockSpec((1, d_model), lambda i, f: (0, 0))],
        out_specs=pl.BlockSpec((tm, d_model), row),
        out_shape=jax.ShapeDtypeStruct((n_tok, d_model), F32),
        compiler_params=_params(("parallel", "arbitrary"), 56),
        name="ffn",
    )(n2, hid, w_gate, w_up, w_down, final_w)


def kernel(x, norm_mix_w, w_in, ret_decay_fwd, ret_decay_bwd, ret_norm_w, w_out,
           norm_ffn_w, w_gate, w_up, w_down, norm_final_w):
    batch, seq, d_model = x.shape
    depth = w_in.shape[0]
    assert depth >= 1
    slopes = jnp.exp2(-8.0 * jnp.arange(1, N_HEADS + 1, dtype=F32) / N_HEADS)
    final_w = norm_final_w.reshape(1, d_model)

    hid = x.reshape(batch * seq, d_model)
    for layer in range(depth):
        w_in_l = w_in[layer].astype(BF16)
        nw = norm_mix_w[layer].reshape(1, d_model)
        qkv_a, qkvg_r = _inproj(hid, nw, w_in_l, 3, F32, BF16, batch, seq)
        attn = _attention(slopes, qkv_a)
        ret = _retention(ret_decay_fwd[layer], ret_decay_bwd[layer], qkvg_r,
                         ret_norm_w[layer].reshape(N_HEADS, 1, HEAD_DIM))
        hid, n2 = _outproj(attn, ret, hid, w_out[layer].astype(BF16),
                           norm_ffn_w[layer].reshape(1, d_model))
        last = layer == depth - 1
        hid = _ffn(n2, hid, w_gate[layer].astype(BF16), w_up[layer].astype(BF16),
                   w_down[layer].astype(BF16), final_w, final_norm=last)
    return hid.reshape(batch, seq, d_model)
```

```python
import functools

import jax
import jax.numpy as jnp
from jax import lax
from jax.experimental import pallas as pl
from jax.experimental.pallas import tpu as pltpu

F32 = jnp.float32
BF16 = jnp.bfloat16

HEAD_DIM = 128
N_HEADS = 8
D_GROUP = N_HEADS * HEAD_DIM
DILATED_PATTERNS = ((128, 1), (512, 4), (2048, 16))
HALF_BAND = 64
RET_CHUNK = 128
EPS = 1e-6
NEG_BIG = -1e30
LOG2_E = 1.4426950408889634
MIB = 1024 * 1024

FFN_TF = 512
ATT_TQ = 128
ATT_TK = ATT_TQ + 2 * HALF_BAND


def _params(semantics, vmem_mib, flags=None):
    return pltpu.CompilerParams(dimension_semantics=semantics,
                                vmem_limit_bytes=vmem_mib * MIB, flags=flags)


def _to_bf16(x, mul=None):
    return (x if mul is None else x * mul).astype(BF16)


def _rmsnorm_rows(x, w):
    return x * lax.rsqrt(jnp.mean(x * x, axis=-1, keepdims=True) + EPS) * w


def _inproj_kernel(x_ref, nw_ref, w_ref, oa_ref, ob_ref, n_scr, *, groups_a):
    j = pl.program_id(1)

    @pl.when(j == 0)
    def _():
        n_scr[...] = _rmsnorm_rows(x_ref[...], nw_ref[...]).astype(BF16)
        ob_ref[...] = jnp.zeros_like(ob_ref)

    def project(o_ref):
        res = jnp.dot(n_scr[...], w_ref[...], preferred_element_type=F32)
        for h in range(N_HEADS):
            o_ref[h] = res[:, h * HEAD_DIM:(h + 1) * HEAD_DIM].astype(o_ref.dtype)

    @pl.when(j < groups_a)
    def _():
        project(oa_ref)

    @pl.when(j >= groups_a)
    def _():
        project(ob_ref)


def _inproj(x2, norm_w, w, groups_a, dtype_a, dtype_b, batch, seq, tm=1024):
    n_tok, d_model = x2.shape
    groups = w.shape[1] // D_GROUP
    groups_b = groups - groups_a
    s_blocks = seq // tm

    def out_spec(group_of_j):
        return pl.BlockSpec(
            (None, None, N_HEADS, tm, HEAD_DIM),
            lambda i, j: (group_of_j(j), i // s_blocks, 0, i % s_blocks, 0))

    def out_shape(n_groups, dtype):
        return jax.ShapeDtypeStruct((n_groups, batch, N_HEADS, seq, HEAD_DIM), dtype)

    return pl.pallas_call(
        functools.partial(_inproj_kernel, groups_a=groups_a),
        grid=(n_tok // tm, groups),
        in_specs=[
            pl.BlockSpec((tm, d_model), lambda i, j: (i, 0)),
            pl.BlockSpec((1, d_model), lambda i, j: (0, 0)),
            pl.BlockSpec((d_model, D_GROUP), lambda i, j: (0, j)),
        ],
        out_specs=[out_spec(lambda j: jnp.minimum(j, groups_a - 1)),
                   out_spec(lambda j: jnp.maximum(j - groups_a, 0))],
        out_shape=[out_shape(groups_a, dtype_a), out_shape(groups_b, dtype_b)],
        scratch_shapes=[pltpu.VMEM((tm, d_model), BF16)],
        compiler_params=_params(("parallel", "arbitrary"), 56),
        name="inproj",
    )(x2, norm_w, w)


def _attn_kernel(slopes_ref, q_ref, k_ref, v_ref, wg_ref, wu_ref, wd_ref,
                 o_ref, wg_bf, wu_bf, wd_bf,
                 q4f, k4f, v4f, qs, ks, vs,
                 num4, den4, m4, num1, den1, m1, bias_s):
    for w_src, w_dst in ((wg_ref, wg_bf), (wu_ref, wu_bf), (wd_ref, wd_bf)):
        w_dst[...] = w_src[...].astype(BF16)

    seq = q_ref.shape[0]
    quarter = seq // 4
    q_scale = HEAD_DIM ** -0.5 * LOG2_E

    @pl.when(pl.program_id(1) == 0)
    def _():
        slope = slopes_ref[pl.program_id(0)]
        row = lax.broadcasted_iota(jnp.int32, (ATT_TQ, ATT_TK), 0)
        col = lax.broadcasted_iota(jnp.int32, (ATT_TQ, ATT_TK), 1)
        for p, (_, dil) in enumerate(DILATED_PATTERNS):
            for c, off in enumerate((0, HALF_BAND, 2 * HALF_BAND)):
                dist = jnp.abs(row - col + off)
                bias = -slope * (dil * dist).astype(F32) * LOG2_E
                bias_s[p * 3 + c] = jnp.where(dist <= HALF_BAND, bias, NEG_BIG)
        vs[:, HEAD_DIM:] = jnp.ones((seq, HEAD_DIM), BF16)

    for src, dst in ((q_ref, q4f), (k_ref, k4f), (v_ref, v4f)):
        for r in range(4):
            dst[r * quarter:(r + 1) * quarter, :] = src[pl.ds(r, quarter, stride=4), :]

    def run_pattern(p, dil, incoming, emit):
        sub_len = seq // dil
        n_blk = sub_len // ATT_TQ
        shift = n_blk.bit_length() - 1
        assert n_blk == 1 << shift and n_blk >= 2

        def block(idx, carry):
            r = idx >> shift
            blk = idx & (n_blk - 1)
            q0 = blk * ATT_TQ
            k0 = jnp.clip(q0 - HALF_BAND, 0, sub_len - ATT_TK)
            q_rows = pl.ds(pl.multiple_of(r * sub_len + q0, ATT_TQ), ATT_TQ)
            k_rows = pl.ds(pl.multiple_of(r * sub_len + k0, HALF_BAND), ATT_TK)
            case = jnp.where(blk == 0, 0, jnp.where(blk == n_blk - 1, 2, 1))
            s = lax.dot_general(qs[q_rows, :], ks[k_rows, :], (((1,), (1,)), ((), ())),
                                preferred_element_type=F32)
            s = s + bias_s[p * 3 + case]
            m_blk = jnp.max(s, axis=-1, keepdims=True)
            if incoming is None:
                m_new = jnp.broadcast_to(m_blk, (ATT_TQ, HEAD_DIM))
            else:
                num_in, den_in, m_in = incoming
                m_old = m_in[q_rows, :]
                m_new = jnp.maximum(m_old, m_blk)
            e = jnp.exp2(s - jnp.concatenate([m_new, m_new], axis=-1))
            pv = jnp.dot(e.astype(BF16), vs[k_rows, :], preferred_element_type=F32)
            num = pv[:, :HEAD_DIM]
            den = pv[:, HEAD_DIM:]
            if incoming is not None:
                alpha = jnp.exp2(m_old - m_new)
                num = alpha * num_in[q_rows, :] + num
                den = alpha * den_in[q_rows, :] + den
            emit(r, q0, num, den, m_new)
            return carry

        lax.fori_loop(0, dil * n_blk, block, 0, unroll=16)

    sub16 = seq // 16
    for src, dst, mul in ((q4f, qs, q_scale), (k4f, ks, None), (v4f, vs, None)):
        for r16 in range(16):
            r4, b = r16 % 4, r16 // 4
            dst[r16 * sub16:(r16 + 1) * sub16, :HEAD_DIM] = _to_bf16(
                src[pl.ds(r4 * quarter + b, sub16, stride=4), :], mul)

    def emit16(r, q0, num, den, m_new):
        rows = pl.ds((r & 3) * quarter + 4 * q0 + (r >> 2), ATT_TQ, stride=4)
        num4[rows, :] = num
        den4[rows, :] = den
        m4[rows, :] = m_new

    run_pattern(2, 16, None, emit16)

    for src, dst, mul in ((q4f, qs, q_scale), (k4f, ks, None), (v4f, vs, None)):
        dst[:, :HEAD_DIM] = _to_bf16(src[...], mul)

    def emit4(r, q0, num, den, m_new):
        rows = pl.ds(4 * q0 + r, ATT_TQ, stride=4)
        num1[rows, :] = num
        den1[rows, :] = den
        m1[rows, :] = m_new

    run_pattern(1, 4, (num4, den4, m4), emit4)

    for src, dst, mul in ((q_ref, qs, q_scale), (k_ref, ks, None), (v_ref, vs, None)):
        dst[:, :HEAD_DIM] = _to_bf16(src[...], mul)

    def emit1(r, q0, num, den, m_new):
        o_ref[pl.ds(pl.multiple_of(q0, ATT_TQ), ATT_TQ), :] = (num / den).astype(o_ref.dtype)

    run_pattern(0, 1, (num1, den1, m1), emit1)


def _cast_specs(weights, n_steps, step_of):
    specs, shapes = [], []
    for w in weights:
        rows = w.shape[0] // n_steps
        assert rows * n_steps == w.shape[0] and rows % 16 == 0
        specs.append(pl.BlockSpec((rows, w.shape[1]), lambda *idx: (step_of(*idx), 0)))
        shapes.append(jax.ShapeDtypeStruct(w.shape, BF16))
    return specs, shapes


def _attention(slopes, qkv, ffn_weights):
    _, batch, heads, seq, dh = qkv.shape
    assert seq % (16 * ATT_TK) == 0 and dh == HEAD_DIM

    def spec(t):
        return pl.BlockSpec((None, None, None, seq, dh),
                            lambda h, b, t=t: (t, b, h, 0, 0))

    w_specs, w_shapes = _cast_specs(ffn_weights, heads * batch, lambda h, b: h * batch + b)
    f32_rows = pltpu.VMEM((seq, dh), F32)
    return pl.pallas_call(
        _attn_kernel,
        grid=(heads, batch),
        in_specs=[pl.BlockSpec(memory_space=pltpu.SMEM), spec(0), spec(1), spec(2)] + w_specs,
        out_specs=[pl.BlockSpec((None, None, seq, dh), lambda h, b: (b, h, 0, 0))] + w_specs,
        out_shape=[jax.ShapeDtypeStruct((batch, heads, seq, dh), BF16)] + w_shapes,
        scratch_shapes=[
            f32_rows, f32_rows, f32_rows,
            pltpu.VMEM((seq, dh), BF16), pltpu.VMEM((seq, dh), BF16),
            pltpu.VMEM((seq, 2 * dh), BF16),
            f32_rows, f32_rows, f32_rows, f32_rows, f32_rows, f32_rows,
            pltpu.VMEM((3 * len(DILATED_PATTERNS), ATT_TQ, ATT_TK), F32),
        ],
        compiler_params=_params(("parallel", "arbitrary"), 56),
        name="dilated_attention",
    )(slopes, qkv, qkv, qkv, *ffn_weights)


def _ret_kernel(df_ref, db_ref, q_ref, k_ref, v_ref, g_ref, rw_ref, wo_ref,
                o_ref, wo_bf, tab, kv_s, st_s):
    wo_bf[...] = wo_ref[...].astype(BF16)

    seq = q_ref.shape[0]
    ch = RET_CHUNK
    n_ch = seq // ch
    h = pl.program_id(1)
    scale = HEAD_DIM ** -0.5
    sq = (ch, HEAD_DIM)
    T_INTRA, T_QF, T_QB, T_KF, T_KB = range(5)

    lg_f = -jnp.exp(jnp.full((1, HEAD_DIM), df_ref[h], F32))
    lg_b = -jnp.exp(jnp.full((1, HEAD_DIM), db_ref[h], F32))
    ri = lax.broadcasted_iota(jnp.int32, sq, 0).astype(F32)
    ci = lax.broadcasted_iota(jnp.int32, sq, 1).astype(F32)
    rel = ri - ci
    tab[T_INTRA] = jnp.where(rel >= 0.0,
                             jnp.exp(lg_f * jnp.maximum(rel, 0.0)),
                             jnp.exp(lg_b * jnp.maximum(-rel, 0.0))) * scale
    tab[T_QF] = jnp.exp(lg_f * (ri + 1.0)) * scale
    tab[T_QB] = jnp.exp(lg_b * (ch - ri)) * scale
    tab[T_KF] = jnp.exp(lg_f * (ch - 1.0 - ri))
    tab[T_KB] = jnp.exp(lg_b * ri)
    cdec_f = jnp.exp(lg_f * ch)
    cdec_b = jnp.exp(lg_b * ch)

    def chunk_rows(c):
        return pl.ds(pl.multiple_of(c * ch, ch), ch)

    def kv_step(c, carry):
        rows = chunk_rows(c)
        vc = v_ref[rows, :].astype(F32)
        v2 = jnp.concatenate([vc * tab[T_KF], vc * tab[T_KB]], axis=-1).astype(BF16)
        kv_s[c] = lax.dot_general(k_ref[rows, :], v2, (((0,), (0,)), ((), ())),
                                  preferred_element_type=F32)
        return carry

    lax.fori_loop(0, n_ch, kv_step, 0, unroll=8)

    def scan_f(c, state):
        st_s[c, :, :HEAD_DIM] = state.astype(BF16)
        return state * cdec_f + kv_s[c, :, :HEAD_DIM]

    def scan_b(i, state):
        c = n_ch - 1 - i
        st_s[c, :, HEAD_DIM:] = state.astype(BF16)
        return state * cdec_b + kv_s[c, :, HEAD_DIM:]

    lax.fori_loop(0, n_ch, scan_f, jnp.zeros(sq, F32), unroll=4)
    lax.fori_loop(0, n_ch, scan_b, jnp.zeros(sq, F32), unroll=4)

    def out_step(c, carry):
        rows = chunk_rows(c)
        qb = q_ref[rows, :]
        inner = lax.dot_general(qb, k_ref[rows, :], (((1,), (1,)), ((), ())),
                                preferred_element_type=F32) * tab[T_INTRA]
        o = jnp.dot(inner.astype(BF16), v_ref[rows, :], preferred_element_type=F32)
        cross = jnp.dot(qb, st_s[c], preferred_element_type=F32)
        o = o + cross[:, :HEAD_DIM] * tab[T_QF] + cross[:, HEAD_DIM:] * tab[T_QB]
        o = o * lax.rsqrt(jnp.mean(o * o, axis=-1, keepdims=True) + EPS)
        o = o * rw_ref[...]
        g = g_ref[rows, :].astype(F32)
        o = o * (g * (1.0 / (1.0 + jnp.exp(-g))))
        o_ref[rows, :] = o.astype(o_ref.dtype)
        return carry

    lax.fori_loop(0, n_ch, out_step, 0, unroll=32)


def _retention(decay_f, decay_b, qkvg, ret_norm_w, w_out):
    _, batch, heads, seq, dh = qkvg.shape

    def spec(t):
        return pl.BlockSpec((None, None, None, seq, dh),
                            lambda b, h, t=t: (t, b, h, 0, 0))

    smem = pl.BlockSpec(memory_space=pltpu.SMEM)
    w_specs, w_shapes = _cast_specs([w_out], batch * heads, lambda b, h: b * heads + h)
    return pl.pallas_call(
        _ret_kernel,
        grid=(batch, heads),
        in_specs=[smem, smem, spec(0), spec(1), spec(2), spec(3),
                  pl.BlockSpec((None, 1, dh), lambda b, h: (h, 0, 0))] + w_specs,
        out_specs=[pl.BlockSpec((None, None, seq, dh), lambda b, h: (b, h, 0, 0))] + w_specs,
        out_shape=[jax.ShapeDtypeStruct((batch, heads, seq, dh), BF16)] + w_shapes,
        scratch_shapes=[pltpu.VMEM((5, RET_CHUNK, dh), F32),
                        pltpu.VMEM((seq // RET_CHUNK, dh, 2 * dh), F32),
                        pltpu.VMEM((seq // RET_CHUNK, dh, 2 * dh), BF16)],
        compiler_params=_params(("parallel", "parallel"), 32),
        name="retention",
    )(decay_f, decay_b, qkvg, qkvg, qkvg, qkvg, ret_norm_w, w_out)


def _outproj_kernel(a_ref, r_ref, x_ref, w_ref, nw_ref, h_ref, n_ref):
    mixed = jnp.concatenate([a_ref[h] for h in range(N_HEADS)]
                            + [r_ref[h] for h in range(N_HEADS)], axis=-1)
    hid = x_ref[...] + jnp.dot(mixed, w_ref[...], preferred_element_type=F32)
    h_ref[...] = hid
    n_ref[...] = _rmsnorm_rows(hid, nw_ref[...]).astype(n_ref.dtype)


def _outproj(attn, ret, x2, w_out, norm_w, tm=512):
    batch, heads, seq, dh = attn.shape
    n_tok, d_model = x2.shape
    s_blocks = seq // tm
    head_spec = pl.BlockSpec((None, heads, tm, dh),
                             lambda i: (i // s_blocks, 0, i % s_blocks, 0))
    row_spec = pl.BlockSpec((tm, d_model), lambda i: (i, 0))
    return pl.pallas_call(
        _outproj_kernel,
        grid=(n_tok // tm,),
        in_specs=[head_spec, head_spec, row_spec,
                  pl.BlockSpec(w_out.shape, lambda i: (0, 0)),
                  pl.BlockSpec((1, d_model), lambda i: (0, 0))],
        out_specs=[row_spec, row_spec],
        out_shape=[jax.ShapeDtypeStruct((n_tok, d_model), F32),
                   jax.ShapeDtypeStruct((n_tok, d_model), BF16)],
        compiler_params=_params(("parallel",), 56),
        name="outproj",
    )(attn, ret, x2, w_out, norm_w)


def _ffn_kernel(n_ref, h_ref, wg_ref, wu_ref, wd_ref, fw_ref, o_ref, *,
                final_norm, h_slices):
    f = pl.program_id(1)
    h_rows = h_ref.shape[0]

    @pl.when(f == 0)
    def _():
        o_ref[...] = jnp.zeros_like(o_ref)

    n = n_ref[...]
    g = jnp.dot(n, wg_ref[...], preferred_element_type=F32)
    u = jnp.dot(n, wu_ref[...], preferred_element_type=F32)
    a = (g * (1.0 / (1.0 + jnp.exp(-g))) * u).astype(BF16)
    o_ref[...] += jnp.dot(a, wd_ref[...], preferred_element_type=F32)

    slice_idx = jnp.minimum(f, h_slices - 1)
    rows = pl.ds(pl.multiple_of(slice_idx * h_rows, h_rows), h_rows)
    o_ref[rows, :] += h_ref[...] * (f < h_slices).astype(F32)

    if final_norm:
        @pl.when(f == pl.num_programs(1) - 1)
        def _():
            o_ref[...] = _rmsnorm_rows(o_ref[...], fw_ref[...])


def _ffn(n2, hid, w_gate, w_up, w_down, final_w, final_norm, tm=1024, tf=FFN_TF,
         h_slices=8):
    n_tok, d_model = hid.shape
    f_steps = w_gate.shape[1] // tf
    assert f_steps >= h_slices and tm % h_slices == 0
    row = lambda i, f: (i, 0)
    col_tile = pl.BlockSpec((d_model, tf), lambda i, f: (0, f))
    return pl.pallas_call(
        functools.partial(_ffn_kernel, final_norm=final_norm, h_slices=h_slices),
        grid=(n_tok // tm, f_steps),
        in_specs=[pl.BlockSpec((tm, d_model), row),
                  pl.BlockSpec((tm // h_slices, d_model),
                               lambda i, f: (i * h_slices + jnp.minimum(f, h_slices - 1), 0)),
                  col_tile, col_tile,
                  pl.BlockSpec((tf, d_model), lambda i, f: (f, 0)),
                  pl.BlockSpec((1, d_model), lambda i, f: (0, 0))],
        out_specs=pl.BlockSpec((tm, d_model), row),
        out_shape=jax.ShapeDtypeStruct((n_tok, d_model), F32),
        compiler_params=_params(("parallel", "arbitrary"), 56),
        name="ffn",
    )(n2, hid, w_gate, w_up, w_down, final_w)


def kernel(x, norm_mix_w, w_in, ret_decay_fwd, ret_decay_bwd, ret_norm_w, w_out,
           norm_ffn_w, w_gate, w_up, w_down, norm_final_w):
    batch, seq, d_model = x.shape
    depth = w_in.shape[0]
    assert depth >= 1
    slopes = jnp.exp2(-8.0 * jnp.arange(1, N_HEADS + 1, dtype=F32) / N_HEADS)
    final_w = norm_final_w.reshape(1, d_model)

    hid = x.reshape(batch * seq, d_model)
    for layer in range(depth):
        w_in_l = w_in[layer].astype(BF16)
        nw = norm_mix_w[layer].reshape(1, d_model)
        qkv_a, qkvg_r = _inproj(hid, nw, w_in_l, 3, F32, BF16, batch, seq)
        attn, w_gate_l, w_up_l, w_down_l = _attention(
            slopes, qkv_a, [w_gate[layer], w_up[layer], w_down[layer]])
        ret, w_out_l = _retention(ret_decay_fwd[layer], ret_decay_bwd[layer], qkvg_r,
                                  ret_norm_w[layer].reshape(N_HEADS, 1, HEAD_DIM),
                                  w_out[layer])
        hid, n2 = _outproj(attn, ret, hid, w_out_l, norm_ffn_w[layer].reshape(1, d_model))
        last = layer == depth - 1
        hid = _ffn(n2, hid, w_gate_l, w_up_l, w_down_l, final_w, final_norm=last)
    return hid.reshape(batch, seq, d_model)
```

```python
import functools

import jax
import jax.numpy as jnp
from jax import lax
from jax.experimental import pallas as pl
from jax.experimental.pallas import tpu as pltpu

F32 = jnp.float32
BF16 = jnp.bfloat16

HEAD_DIM = 128
N_HEADS = 8
D_GROUP = N_HEADS * HEAD_DIM
DILATED_PATTERNS = ((128, 1), (512, 4), (2048, 16))
HALF_BAND = 64
RET_BLOCK = 256
EPS = 1e-6
NEG_BIG = -1e30
LOG2_E = 1.4426950408889634
MIB = 1024 * 1024

FFN_TF = 512
ATT_TQ = 128
ATT_TK = ATT_TQ + 2 * HALF_BAND


def _params(semantics, vmem_mib, flags=None):
    return pltpu.CompilerParams(dimension_semantics=semantics,
                                vmem_limit_bytes=vmem_mib * MIB, flags=flags)


def _to_bf16(x, mul=None):
    return (x if mul is None else x * mul).astype(BF16)


def _rmsnorm_rows(x, w):
    return x * lax.rsqrt(jnp.mean(x * x, axis=-1, keepdims=True) + EPS) * w


def _inproj_kernel(x_ref, nw_ref, w_ref, oa_ref, ob_ref, n_scr, *, groups_a):
    j = pl.program_id(1)

    @pl.when(j == 0)
    def _():
        n_scr[...] = _rmsnorm_rows(x_ref[...], nw_ref[...]).astype(BF16)
        ob_ref[...] = jnp.zeros_like(ob_ref)

    def project(o_ref):
        res = jnp.dot(n_scr[...], w_ref[...], preferred_element_type=F32)
        for h in range(N_HEADS):
            o_ref[h] = res[:, h * HEAD_DIM:(h + 1) * HEAD_DIM].astype(o_ref.dtype)

    @pl.when(j < groups_a)
    def _():
        project(oa_ref)

    @pl.when(j >= groups_a)
    def _():
        project(ob_ref)


def _inproj(x2, norm_w, w, groups_a, dtype_a, dtype_b, batch, seq, tm=1024):
    n_tok, d_model = x2.shape
    groups = w.shape[1] // D_GROUP
    groups_b = groups - groups_a
    s_blocks = seq // tm

    def out_spec(group_of_j):
        return pl.BlockSpec(
            (None, None, N_HEADS, tm, HEAD_DIM),
            lambda i, j: (group_of_j(j), i // s_blocks, 0, i % s_blocks, 0))

    def out_shape(n_groups, dtype):
        return jax.ShapeDtypeStruct((n_groups, batch, N_HEADS, seq, HEAD_DIM), dtype)

    return pl.pallas_call(
        functools.partial(_inproj_kernel, groups_a=groups_a),
        grid=(n_tok // tm, groups),
        in_specs=[
            pl.BlockSpec((tm, d_model), lambda i, j: (i, 0)),
            pl.BlockSpec((1, d_model), lambda i, j: (0, 0)),
            pl.BlockSpec((d_model, D_GROUP), lambda i, j: (0, j)),
        ],
        out_specs=[out_spec(lambda j: jnp.minimum(j, groups_a - 1)),
                   out_spec(lambda j: jnp.maximum(j - groups_a, 0))],
        out_shape=[out_shape(groups_a, dtype_a), out_shape(groups_b, dtype_b)],
        scratch_shapes=[pltpu.VMEM((tm, d_model), BF16)],
        compiler_params=_params(("parallel", "arbitrary"), 56),
        name="inproj",
    )(x2, norm_w, w)


def _attn_kernel(slopes_ref, q_ref, k_ref, v_ref, wg_ref, wu_ref, wd_ref,
                 o_ref, wg_bf, wu_bf, wd_bf,
                 q4f, k4f, v4f, qs, ks, vs,
                 num4, den4, m4, num1, den1, m1, bias_s):
    for w_src, w_dst in ((wg_ref, wg_bf), (wu_ref, wu_bf), (wd_ref, wd_bf)):
        w_dst[...] = w_src[...].astype(BF16)

    seq = q_ref.shape[0]
    quarter = seq // 4
    q_scale = HEAD_DIM ** -0.5 * LOG2_E

    @pl.when(pl.program_id(1) == 0)
    def _():
        slope = slopes_ref[pl.program_id(0)]
        row = lax.broadcasted_iota(jnp.int32, (ATT_TQ, ATT_TK), 0)
        col = lax.broadcasted_iota(jnp.int32, (ATT_TQ, ATT_TK), 1)
        for p, (_, dil) in enumerate(DILATED_PATTERNS):
            for c, off in enumerate((0, HALF_BAND, 2 * HALF_BAND)):
                dist = jnp.abs(row - col + off)
                bias = -slope * (dil * dist).astype(F32) * LOG2_E
                bias_s[p * 3 + c] = jnp.where(dist <= HALF_BAND, bias, NEG_BIG)
        vs[:, HEAD_DIM:] = jnp.ones((seq, HEAD_DIM), BF16)

    for src, dst in ((q_ref, q4f), (k_ref, k4f), (v_ref, v4f)):
        for r in range(4):
            dst[r * quarter:(r + 1) * quarter, :] = src[pl.ds(r, quarter, stride=4), :]

    def run_pattern(p, dil, incoming, emit):
        sub_len = seq // dil
        n_blk = sub_len // ATT_TQ
        shift = n_blk.bit_length() - 1
        assert n_blk == 1 << shift and n_blk >= 2

        def block(idx, carry):
            r = idx >> shift
            blk = idx & (n_blk - 1)
            q0 = blk * ATT_TQ
            k0 = jnp.clip(q0 - HALF_BAND, 0, sub_len - ATT_TK)
            q_rows = pl.ds(pl.multiple_of(r * sub_len + q0, ATT_TQ), ATT_TQ)
            k_rows = pl.ds(pl.multiple_of(r * sub_len + k0, HALF_BAND), ATT_TK)
            case = jnp.where(blk == 0, 0, jnp.where(blk == n_blk - 1, 2, 1))
            s = lax.dot_general(qs[q_rows, :], ks[k_rows, :], (((1,), (1,)), ((), ())),
                                preferred_element_type=F32)
            s = s + bias_s[p * 3 + case]
            m_blk = jnp.max(s, axis=-1, keepdims=True)
            if incoming is None:
                m_new = jnp.broadcast_to(m_blk, (ATT_TQ, HEAD_DIM))
            else:
                num_in, den_in, m_in = incoming
                m_old = m_in[q_rows, :]
                m_new = jnp.maximum(m_old, m_blk)
            e = jnp.exp2(s - jnp.concatenate([m_new, m_new], axis=-1))
            pv = jnp.dot(e.astype(BF16), vs[k_rows, :], preferred_element_type=F32)
            num = pv[:, :HEAD_DIM]
            den = pv[:, HEAD_DIM:]
            if incoming is not None:
                alpha = jnp.exp2(m_old - m_new)
                num = alpha * num_in[q_rows, :] + num
                den = alpha * den_in[q_rows, :] + den
            emit(r, q0, num, den, m_new)
            return carry

        lax.fori_loop(0, dil * n_blk, block, 0, unroll=16)

    sub16 = seq // 16
    for src, dst, mul in ((q4f, qs, q_scale), (k4f, ks, None), (v4f, vs, None)):
        for r16 in range(16):
            r4, b = r16 % 4, r16 // 4
            dst[r16 * sub16:(r16 + 1) * sub16, :HEAD_DIM] = _to_bf16(
                src[pl.ds(r4 * quarter + b, sub16, stride=4), :], mul)

    def emit16(r, q0, num, den, m_new):
        rows = pl.ds((r & 3) * quarter + 4 * q0 + (r >> 2), ATT_TQ, stride=4)
        num4[rows, :] = num
        den4[rows, :] = den
        m4[rows, :] = m_new

    run_pattern(2, 16, None, emit16)

    for src, dst, mul in ((q4f, qs, q_scale), (k4f, ks, None), (v4f, vs, None)):
        dst[:, :HEAD_DIM] = _to_bf16(src[...], mul)

    def emit4(r, q0, num, den, m_new):
        rows = pl.ds(4 * q0 + r, ATT_TQ, stride=4)
        num1[rows, :] = num
        den1[rows, :] = den
        m1[rows, :] = m_new

    run_pattern(1, 4, (num4, den4, m4), emit4)

    for src, dst, mul in ((q_ref, qs, q_scale), (k_ref, ks, None), (v_ref, vs, None)):
        dst[:, :HEAD_DIM] = _to_bf16(src[...], mul)

    def emit1(r, q0, num, den, m_new):
        o_ref[pl.ds(pl.multiple_of(q0, ATT_TQ), ATT_TQ), :] = (num / den).astype(o_ref.dtype)

    run_pattern(0, 1, (num1, den1, m1), emit1)


def _cast_specs(weights, n_steps, step_of):
    specs, shapes = [], []
    for w in weights:
        rows = w.shape[0] // n_steps
        assert rows * n_steps == w.shape[0] and rows % 16 == 0
        specs.append(pl.BlockSpec((rows, w.shape[1]), lambda *idx: (step_of(*idx), 0)))
        shapes.append(jax.ShapeDtypeStruct(w.shape, BF16))
    return specs, shapes


def _attention(slopes, qkv, ffn_weights):
    _, batch, heads, seq, dh = qkv.shape
    assert seq % (16 * ATT_TK) == 0 and dh == HEAD_DIM

    def spec(t):
        return pl.BlockSpec((None, None, None, seq, dh),
                            lambda h, b, t=t: (t, b, h, 0, 0))

    w_specs, w_shapes = _cast_specs(ffn_weights, heads * batch, lambda h, b: h * batch + b)
    f32_rows = pltpu.VMEM((seq, dh), F32)
    return pl.pallas_call(
        _attn_kernel,
        grid=(heads, batch),
        in_specs=[pl.BlockSpec(memory_space=pltpu.SMEM), spec(0), spec(1), spec(2)] + w_specs,
        out_specs=[pl.BlockSpec((None, None, seq, dh), lambda h, b: (b, h, 0, 0))] + w_specs,
        out_shape=[jax.ShapeDtypeStruct((batch, heads, seq, dh), BF16)] + w_shapes,
        scratch_shapes=[
            f32_rows, f32_rows, f32_rows,
            pltpu.VMEM((seq, dh), BF16), pltpu.VMEM((seq, dh), BF16),
            pltpu.VMEM((seq, 2 * dh), BF16),
            f32_rows, f32_rows, f32_rows, f32_rows, f32_rows, f32_rows,
            pltpu.VMEM((3 * len(DILATED_PATTERNS), ATT_TQ, ATT_TK), F32),
        ],
        compiler_params=_params(("parallel", "arbitrary"), 56),
        name="dilated_attention",
    )(slopes, qkv, qkv, qkv, *ffn_weights)


def _ret_kernel(df_ref, db_ref, q_ref, k_ref, v_ref, g_ref, rw_ref, wo_ref,
                o_ref, wo_bf, intra_s, tab, kv_s, st_s):
    wo_bf[...] = wo_ref[...].astype(BF16)

    seq = q_ref.shape[0]
    ch = RET_BLOCK
    n_ch = seq // ch
    h = pl.program_id(0)
    scale = HEAD_DIM ** -0.5
    T_QF, T_QB, T_KF, T_KB = range(4)

    lg_f = -jnp.exp(jnp.full((1, HEAD_DIM), df_ref[h], F32))
    lg_b = -jnp.exp(jnp.full((1, HEAD_DIM), db_ref[h], F32))

    @pl.when(pl.program_id(1) == 0)
    def _():
        rel = (lax.broadcasted_iota(jnp.int32, (ch, ch), 0)
               - lax.broadcasted_iota(jnp.int32, (ch, ch), 1)).astype(F32)
        intra_s[...] = jnp.where(rel >= 0.0,
                                 jnp.exp(lg_f[:, :1] * jnp.maximum(rel, 0.0)),
                                 jnp.exp(lg_b[:, :1] * jnp.maximum(-rel, 0.0))) * scale
        ri = lax.broadcasted_iota(jnp.int32, (ch, HEAD_DIM), 0).astype(F32)
        tab[T_QF] = jnp.exp(lg_f * (ri + 1.0)) * scale
        tab[T_QB] = jnp.exp(lg_b * (ch - ri)) * scale
        tab[T_KF] = jnp.exp(lg_f * (ch - 1.0 - ri))
        tab[T_KB] = jnp.exp(lg_b * ri)

    cdec_f = jnp.exp(lg_f * ch)
    cdec_b = jnp.exp(lg_b * ch)

    def chunk_rows(c):
        return pl.ds(pl.multiple_of(c * ch, ch), ch)

    def kv_step(c, carry):
        rows = chunk_rows(c)
        vc = v_ref[rows, :].astype(F32)
        v2 = jnp.concatenate([vc * tab[T_KF], vc * tab[T_KB]], axis=-1).astype(BF16)
        kv_s[c] = lax.dot_general(k_ref[rows, :], v2, (((0,), (0,)), ((), ())),
                                  preferred_element_type=F32)
        return carry

    lax.fori_loop(0, n_ch, kv_step, 0, unroll=8)
    zero_state = jnp.zeros((HEAD_DIM, HEAD_DIM), F32)

    def scan_f(c, state):
        st_s[c, :, :HEAD_DIM] = state.astype(BF16)
        return state * cdec_f + kv_s[c, :, :HEAD_DIM]

    def scan_b(i, state):
        c = n_ch - 1 - i
        st_s[c, :, HEAD_DIM:] = state.astype(BF16)
        return state * cdec_b + kv_s[c, :, HEAD_DIM:]

    lax.fori_loop(0, n_ch, scan_f, zero_state, unroll=4)
    lax.fori_loop(0, n_ch, scan_b, zero_state, unroll=4)

    def out_step(c, carry):
        rows = chunk_rows(c)
        qb = q_ref[rows, :]
        inner = lax.dot_general(qb, k_ref[rows, :], (((1,), (1,)), ((), ())),
                                preferred_element_type=F32) * intra_s[...]
        o = jnp.dot(inner.astype(BF16), v_ref[rows, :], preferred_element_type=F32)
        cross = jnp.dot(qb, st_s[c], preferred_element_type=F32)
        o = o + cross[:, :HEAD_DIM] * tab[T_QF] + cross[:, HEAD_DIM:] * tab[T_QB]
        o = o * lax.rsqrt(jnp.mean(o * o, axis=-1, keepdims=True) + EPS)
        o = o * rw_ref[...]
        g = g_ref[rows, :].astype(F32)
        o = o * (g * (1.0 / (1.0 + jnp.exp(-g))))
        o_ref[rows, :] = o.astype(o_ref.dtype)
        return carry

    lax.fori_loop(0, n_ch, out_step, 0, unroll=n_ch)


def _retention(decay_f, decay_b, qkvg, ret_norm_w, w_out):
    _, batch, heads, seq, dh = qkvg.shape

    def spec(t):
        return pl.BlockSpec((None, None, None, seq, dh),
                            lambda h, b, t=t: (t, b, h, 0, 0))

    smem = pl.BlockSpec(memory_space=pltpu.SMEM)
    w_specs, w_shapes = _cast_specs([w_out], heads * batch, lambda h, b: h * batch + b)
    return pl.pallas_call(
        _ret_kernel,
        grid=(heads, batch),
        in_specs=[smem, smem, spec(0), spec(1), spec(2), spec(3),
                  pl.BlockSpec((None, 1, dh), lambda h, b: (h, 0, 0))] + w_specs,
        out_specs=[pl.BlockSpec((None, None, seq, dh), lambda h, b: (b, h, 0, 0))] + w_specs,
        out_shape=[jax.ShapeDtypeStruct((batch, heads, seq, dh), BF16)] + w_shapes,
        scratch_shapes=[pltpu.VMEM((RET_BLOCK, RET_BLOCK), F32),
                        pltpu.VMEM((4, RET_BLOCK, dh), F32),
                        pltpu.VMEM((seq // RET_BLOCK, dh, 2 * dh), F32),
                        pltpu.VMEM((seq // RET_BLOCK, dh, 2 * dh), BF16)],
        compiler_params=_params(("parallel", "arbitrary"), 32),
        name="retention",
    )(decay_f, decay_b, qkvg, qkvg, qkvg, qkvg, ret_norm_w, w_out)


def _outproj_kernel(a_ref, r_ref, x_ref, w_ref, nw_ref, h_ref, n_ref):
    half = x_ref.shape[0] // 2
    for c in range(2):
        rows = slice(c * half, (c + 1) * half)
        mixed = jnp.concatenate([a_ref[h, rows, :] for h in range(N_HEADS)]
                                + [r_ref[h, rows, :] for h in range(N_HEADS)], axis=-1)
        hid = x_ref[rows, :] + jnp.dot(mixed, w_ref[...], preferred_element_type=F32)
        h_ref[rows, :] = hid
        n_ref[rows, :] = _rmsnorm_rows(hid, nw_ref[...]).astype(n_ref.dtype)


def _outproj(attn, ret, x2, w_out, norm_w, tm=512):
    batch, heads, seq, dh = attn.shape
    n_tok, d_model = x2.shape
    s_blocks = seq // tm
    head_spec = pl.BlockSpec((None, heads, tm, dh),
                             lambda i: (i // s_blocks, 0, i % s_blocks, 0))
    row_spec = pl.BlockSpec((tm, d_model), lambda i: (i, 0))
    return pl.pallas_call(
        _outproj_kernel,
        grid=(n_tok // tm,),
        in_specs=[head_spec, head_spec, row_spec,
                  pl.BlockSpec(w_out.shape, lambda i: (0, 0)),
                  pl.BlockSpec((1, d_model), lambda i: (0, 0))],
        out_specs=[row_spec, row_spec],
        out_shape=[jax.ShapeDtypeStruct((n_tok, d_model), F32),
                   jax.ShapeDtypeStruct((n_tok, d_model), BF16)],
        compiler_params=_params(("parallel",), 56),
        name="outproj",
    )(attn, ret, x2, w_out, norm_w)


def _ffn_kernel(n_ref, h_ref, wg_ref, wu_ref, wd_ref, fw_ref, o_ref, *,
                final_norm, h_slices):
    f = pl.program_id(1)
    h_rows = h_ref.shape[0]

    @pl.when(f == 0)
    def _():
        o_ref[...] = jnp.zeros_like(o_ref)

    n = n_ref[...]
    g = jnp.dot(n, wg_ref[...], preferred_element_type=F32)
    u = jnp.dot(n, wu_ref[...], preferred_element_type=F32)
    a = (g * (1.0 / (1.0 + jnp.exp(-g))) * u).astype(BF16)
    o_ref[...] += jnp.dot(a, wd_ref[...], preferred_element_type=F32)

    slice_idx = jnp.minimum(f, h_slices - 1)
    rows = pl.ds(pl.multiple_of(slice_idx * h_rows, h_rows), h_rows)
    o_ref[rows, :] += h_ref[...] * (f < h_slices).astype(F32)

    if final_norm:
        @pl.when(f == pl.num_programs(1) - 1)
        def _():
            o_ref[...] = _rmsnorm_rows(o_ref[...], fw_ref[...])


def _ffn(n2, hid, w_gate, w_up, w_down, final_w, final_norm, tm=1024, tf=FFN_TF,
         h_slices=8):
    n_tok, d_model = hid.shape
    f_steps = w_gate.shape[1] // tf
    assert f_steps >= h_slices and tm % h_slices == 0
    row = lambda i, f: (i, 0)
    col_tile = pl.BlockSpec((d_model, tf), lambda i, f: (0, f))
    return pl.pallas_call(
        functools.partial(_ffn_kernel, final_norm=final_norm, h_slices=h_slices),
        grid=(n_tok // tm, f_steps),
        in_specs=[pl.BlockSpec((tm, d_model), row),
                  pl.BlockSpec((tm // h_slices, d_model),
                               lambda i, f: (i * h_slices + jnp.minimum(f, h_slices - 1), 0)),
                  col_tile, col_tile,
                  pl.BlockSpec((tf, d_model), lambda i, f: (f, 0)),
                  pl.BlockSpec((1, d_model), lambda i, f: (0, 0))],
        out_specs=pl.BlockSpec((tm, d_model), row),
        out_shape=jax.ShapeDtypeStruct((n_tok, d_model), F32),
        compiler_params=_params(("parallel", "arbitrary"), 56),
        name="ffn",
    )(n2, hid, w_gate, w_up, w_down, final_w)


def kernel(x, norm_mix_w, w_in, ret_decay_fwd, ret_decay_bwd, ret_norm_w, w_out,
           norm_ffn_w, w_gate, w_up, w_down, norm_final_w):
    batch, seq, d_model = x.shape
    depth = w_in.shape[0]
    assert depth >= 1
    slopes = jnp.exp2(-8.0 * jnp.arange(1, N_HEADS + 1, dtype=F32) / N_HEADS)
    final_w = norm_final_w.reshape(1, d_model)

    hid = x.reshape(batch * seq, d_model)
    for layer in range(depth):
        w_in_l = w_in[layer].astype(BF16)
        nw = norm_mix_w[layer].reshape(1, d_model)
        qkv_a, qkvg_r = _inproj(hid, nw, w_in_l, 3, F32, BF16, batch, seq)
        attn, w_gate_l, w_up_l, w_down_l = _attention(
            slopes, qkv_a, [w_gate[layer], w_up[layer], w_down[layer]])
        ret, w_out_l = _retention(ret_decay_fwd[layer], ret_decay_bwd[layer], qkvg_r,
                                  ret_norm_w[layer].reshape(N_HEADS, 1, HEAD_DIM),
                                  w_out[layer])
        hid, n2 = _outproj(attn, ret, hid, w_out_l, norm_ffn_w[layer].reshape(1, d_model))
        last = layer == depth - 1
        hid = _ffn(n2, hid, w_gate_l, w_up_l, w_down_l, final_w, final_norm=last)
    return hid.reshape(batch, seq, d_model)
```

```python
import functools

import jax
import jax.numpy as jnp
from jax import lax
from jax.experimental import pallas as pl
from jax.experimental.pallas import tpu as pltpu

F32 = jnp.float32
BF16 = jnp.bfloat16

HEAD_DIM = 128
N_HEADS = 8
D_GROUP = N_HEADS * HEAD_DIM
DILATED_PATTERNS = ((128, 1), (512, 4), (2048, 16))
HALF_BAND = 64
RET_BLOCK = 256
EPS = 1e-6
NEG_BIG = -1e30
LOG2_E = 1.4426950408889634
MIB = 1024 * 1024

FFN_TF = 512
ATT_TQ = 128
ATT_TK = ATT_TQ + 2 * HALF_BAND


def _params(semantics, vmem_mib, flags=None):
    return pltpu.CompilerParams(dimension_semantics=semantics,
                                vmem_limit_bytes=vmem_mib * MIB, flags=flags)


def _to_bf16(x, mul=None):
    return (x if mul is None else x * mul).astype(BF16)


def _rmsnorm_rows(x, w):
    return x * lax.rsqrt(jnp.mean(x * x, axis=-1, keepdims=True) + EPS) * w


def _inproj_kernel(x_ref, nw_ref, w_ref, oa_ref, ob_ref, n_scr, *,
                   groups, groups_a, x_slices):
    i = pl.program_id(0)
    j = pl.program_id(1)
    slice_rows = x_ref.shape[0]

    def norm_slice():
        rows = pl.ds(pl.multiple_of(j * slice_rows, slice_rows), slice_rows)
        n_scr[i & 1, rows, :] = _rmsnorm_rows(x_ref[...], nw_ref[...]).astype(BF16)

    def project(o_ref, with_norm):
        res = jnp.dot(n_scr[(i - 1) & 1], w_ref[...], preferred_element_type=F32)
        for h in range(N_HEADS):
            o_ref[h] = res[:, h * HEAD_DIM:(h + 1) * HEAD_DIM].astype(o_ref.dtype)
        if with_norm:
            norm_slice()

    @pl.when(j == 0)
    def _():
        ob_ref[...] = jnp.zeros_like(ob_ref)

    @pl.when((i == 0) & (j == 0))
    def _():
        oa_ref[...] = jnp.zeros_like(oa_ref)

    pl.when((i == 0) & (j < x_slices))(norm_slice)
    cases = {}
    for step in range(groups):
        cases.setdefault((step < groups_a, step < x_slices), []).append(step)
    for (to_a, with_norm), steps in cases.items():
        pl.when((i > 0) & (j >= steps[0]) & (j <= steps[-1]))(
            functools.partial(project, oa_ref if to_a else ob_ref, with_norm))


def _inproj(x2, norm_w, w, groups_a, dtype_a, dtype_b, batch, seq, tm=1024, x_slices=4):
    n_tok, d_model = x2.shape
    groups = w.shape[1] // D_GROUP
    groups_b = groups - groups_a
    n_tiles = n_tok // tm
    s_blocks = seq // tm
    assert x_slices <= groups and tm % x_slices == 0

    def out_spec(group_of_j):
        def index(i, j):
            t = jnp.maximum(i - 1, 0)
            g = jnp.where(i == 0, 0, group_of_j(j))
            return (g, t // s_blocks, 0, t % s_blocks, 0)
        return pl.BlockSpec((None, None, N_HEADS, tm, HEAD_DIM), index)

    def out_shape(n_groups, dtype):
        return jax.ShapeDtypeStruct((n_groups, batch, N_HEADS, seq, HEAD_DIM), dtype)

    return pl.pallas_call(
        functools.partial(_inproj_kernel, groups=groups, groups_a=groups_a,
                          x_slices=x_slices),
        grid=(n_tiles + 1, groups),
        in_specs=[
            pl.BlockSpec((tm // x_slices, d_model),
                         lambda i, j: (jnp.minimum(i, n_tiles - 1) * x_slices
                                       + jnp.minimum(j, x_slices - 1), 0)),
            pl.BlockSpec((1, d_model), lambda i, j: (0, 0)),
            pl.BlockSpec((d_model, D_GROUP), lambda i, j: (0, jnp.where(i == 0, 0, j))),
        ],
        out_specs=[out_spec(lambda j: jnp.minimum(j, groups_a - 1)),
                   out_spec(lambda j: jnp.maximum(j - groups_a, 0))],
        out_shape=[out_shape(groups_a, dtype_a), out_shape(groups_b, dtype_b)],
        scratch_shapes=[pltpu.VMEM((2, tm, d_model), BF16)],
        compiler_params=_params(("arbitrary", "arbitrary"), 48),
        name="inproj",
    )(x2, norm_w, w)


def _attn_kernel(slopes_ref, q_ref, k_ref, v_ref, wg_ref, wu_ref, wd_ref,
                 o_ref, wg_bf, wu_bf, wd_bf,
                 q4f, k4f, v4f, qs, ks, vs,
                 num4, den4, m4, num1, den1, m1, bias_s):
    for w_src, w_dst in ((wg_ref, wg_bf), (wu_ref, wu_bf), (wd_ref, wd_bf)):
        w_dst[...] = w_src[...].astype(BF16)

    seq = q_ref.shape[0]
    quarter = seq // 4
    q_scale = HEAD_DIM ** -0.5 * LOG2_E

    @pl.when(pl.program_id(1) == 0)
    def _():
        slope = slopes_ref[pl.program_id(0)]
        row = lax.broadcasted_iota(jnp.int32, (ATT_TQ, ATT_TK), 0)
        col = lax.broadcasted_iota(jnp.int32, (ATT_TQ, ATT_TK), 1)
        for p, (_, dil) in enumerate(DILATED_PATTERNS):
            for c, off in enumerate((0, HALF_BAND, 2 * HALF_BAND)):
                dist = jnp.abs(row - col + off)
                bias = -slope * (dil * dist).astype(F32) * LOG2_E
                bias_s[p * 3 + c] = jnp.where(dist <= HALF_BAND, bias, NEG_BIG)
        vs[:, HEAD_DIM:] = jnp.ones((seq, HEAD_DIM), BF16)

    for src, dst in ((q_ref, q4f), (k_ref, k4f), (v_ref, v4f)):
        for r in range(4):
            dst[r * quarter:(r + 1) * quarter, :] = src[pl.ds(r, quarter, stride=4), :]

    def run_pattern(p, dil, incoming, emit):
        sub_len = seq // dil
        n_blk = sub_len // ATT_TQ
        shift = n_blk.bit_length() - 1
        assert n_blk == 1 << shift and n_blk >= 2

        def block(idx, carry):
            r = idx >> shift
            blk = idx & (n_blk - 1)
            q0 = blk * ATT_TQ
            k0 = jnp.clip(q0 - HALF_BAND, 0, sub_len - ATT_TK)
            q_rows = pl.ds(pl.multiple_of(r * sub_len + q0, ATT_TQ), ATT_TQ)
            k_rows = pl.ds(pl.multiple_of(r * sub_len + k0, HALF_BAND), ATT_TK)
            case = jnp.where(blk == 0, 0, jnp.where(blk == n_blk - 1, 2, 1))
            s = lax.dot_general(qs[q_rows, :], ks[k_rows, :], (((1,), (1,)), ((), ())),
                                preferred_element_type=F32)
            s = s + bias_s[p * 3 + case]
            m_blk = jnp.max(s, axis=-1, keepdims=True)
            if incoming is None:
                m_new = jnp.broadcast_to(m_blk, (ATT_TQ, HEAD_DIM))
            else:
                num_in, den_in, m_in = incoming
                m_old = m_in[q_rows, :]
                m_new = jnp.maximum(m_old, m_blk)
            e = jnp.exp2(s - jnp.concatenate([m_new, m_new], axis=-1))
            pv = jnp.dot(e.astype(BF16), vs[k_rows, :], preferred_element_type=F32)
            num = pv[:, :HEAD_DIM]
            den = pv[:, HEAD_DIM:]
            if incoming is not None:
                alpha = jnp.exp2(m_old - m_new)
                num = alpha * num_in[q_rows, :] + num
                den = alpha * den_in[q_rows, :] + den
            emit(r, q0, num, den, m_new)
            return carry

        lax.fori_loop(0, dil * n_blk, block, 0, unroll=16)

    sub16 = seq // 16
    for src, dst, mul in ((q4f, qs, q_scale), (k4f, ks, None), (v4f, vs, None)):
        for r16 in range(16):
            r4, b = r16 % 4, r16 // 4
            dst[r16 * sub16:(r16 + 1) * sub16, :HEAD_DIM] = _to_bf16(
                src[pl.ds(r4 * quarter + b, sub16, stride=4), :], mul)

    def emit16(r, q0, num, den, m_new):
        rows = pl.ds((r & 3) * quarter + 4 * q0 + (r >> 2), ATT_TQ, stride=4)
        num4[rows, :] = num
        den4[rows, :] = den
        m4[rows, :] = m_new

    run_pattern(2, 16, None, emit16)

    for src, dst, mul in ((q4f, qs, q_scale), (k4f, ks, None), (v4f, vs, None)):
        dst[:, :HEAD_DIM] = _to_bf16(src[...], mul)

    def emit4(r, q0, num, den, m_new):
        rows = pl.ds(4 * q0 + r, ATT_TQ, stride=4)
        num1[rows, :] = num
        den1[rows, :] = den
        m1[rows, :] = m_new

    run_pattern(1, 4, (num4, den4, m4), emit4)

    for src, dst, mul in ((q_ref, qs, q_scale), (k_ref, ks, None), (v_ref, vs, None)):
        dst[:, :HEAD_DIM] = _to_bf16(src[...], mul)

    def emit1(r, q0, num, den, m_new):
        o_ref[pl.ds(pl.multiple_of(q0, ATT_TQ), ATT_TQ), :] = (num / den).astype(o_ref.dtype)

    run_pattern(0, 1, (num1, den1, m1), emit1)


def _cast_specs(weights, n_steps, step_of):
    specs, shapes = [], []
    for w in weights:
        rows = w.shape[0] // n_steps
        assert rows * n_steps == w.shape[0] and rows % 16 == 0
        specs.append(pl.BlockSpec((rows, w.shape[1]), lambda *idx: (step_of(*idx), 0)))
        shapes.append(jax.ShapeDtypeStruct(w.shape, BF16))
    return specs, shapes


def _attention(slopes, qkv, ffn_weights):
    _, batch, heads, seq, dh = qkv.shape
    assert seq % (16 * ATT_TK) == 0 and dh == HEAD_DIM

    def spec(t):
        return pl.BlockSpec((None, None, None, seq, dh),
                            lambda h, b, t=t: (t, b, h, 0, 0))

    w_specs, w_shapes = _cast_specs(ffn_weights, heads * batch, lambda h, b: h * batch + b)
    f32_rows = pltpu.VMEM((seq, dh), F32)
    return pl.pallas_call(
        _attn_kernel,
        grid=(heads, batch),
        in_specs=[pl.BlockSpec(memory_space=pltpu.SMEM), spec(0), spec(1), spec(2)] + w_specs,
        out_specs=[pl.BlockSpec((None, None, seq, dh), lambda h, b: (b, h, 0, 0))] + w_specs,
        out_shape=[jax.ShapeDtypeStruct((batch, heads, seq, dh), BF16)] + w_shapes,
        scratch_shapes=[
            f32_rows, f32_rows, f32_rows,
            pltpu.VMEM((seq, dh), BF16), pltpu.VMEM((seq, dh), BF16),
            pltpu.VMEM((seq, 2 * dh), BF16),
            f32_rows, f32_rows, f32_rows, f32_rows, f32_rows, f32_rows,
            pltpu.VMEM((3 * len(DILATED_PATTERNS), ATT_TQ, ATT_TK), F32),
        ],
        compiler_params=_params(("parallel", "arbitrary"), 56),
        name="dilated_attention",
    )(slopes, qkv, qkv, qkv, *ffn_weights)


def _ret_kernel(df_ref, db_ref, q_ref, k_ref, v_ref, g_ref, rw_ref, wo_ref,
                o_ref, wo_bf, intra_s, tab, kv_s, st_s):
    wo_bf[...] = wo_ref[...].astype(BF16)

    seq = q_ref.shape[0]
    ch = RET_BLOCK
    n_ch = seq // ch
    h = pl.program_id(0)
    scale = HEAD_DIM ** -0.5
    T_QF, T_QB, T_KF, T_KB = range(4)

    lg_f = -jnp.exp(jnp.full((1, HEAD_DIM), df_ref[h], F32))
    lg_b = -jnp.exp(jnp.full((1, HEAD_DIM), db_ref[h], F32))

    @pl.when(pl.program_id(1) == 0)
    def _():
        rel = (lax.broadcasted_iota(jnp.int32, (ch, ch), 0)
               - lax.broadcasted_iota(jnp.int32, (ch, ch), 1)).astype(F32)
        intra_s[...] = jnp.where(rel >= 0.0,
                                 jnp.exp(lg_f[:, :1] * jnp.maximum(rel, 0.0)),
                                 jnp.exp(lg_b[:, :1] * jnp.maximum(-rel, 0.0))) * scale
        ri = lax.broadcasted_iota(jnp.int32, (ch, HEAD_DIM), 0).astype(F32)
        tab[T_QF] = jnp.exp(lg_f * (ri + 1.0)) * scale
        tab[T_QB] = jnp.exp(lg_b * (ch - ri)) * scale
        tab[T_KF] = jnp.exp(lg_f * (ch - 1.0 - ri))
        tab[T_KB] = jnp.exp(lg_b * ri)

    cdec_f = jnp.exp(lg_f * ch)
    cdec_b = jnp.exp(lg_b * ch)

    def chunk_rows(c):
        return pl.ds(pl.multiple_of(c * ch, ch), ch)

    def kv_step(c, carry):
        rows = chunk_rows(c)
        vc = v_ref[rows, :].astype(F32)
        v2 = jnp.concatenate([vc * tab[T_KF], vc * tab[T_KB]], axis=-1).astype(BF16)
        kv_s[c] = lax.dot_general(k_ref[rows, :], v2, (((0,), (0,)), ((), ())),
                                  preferred_element_type=F32)
        return carry

    lax.fori_loop(0, n_ch, kv_step, 0, unroll=8)
    zero_state = jnp.zeros((HEAD_DIM, HEAD_DIM), F32)

    def scan_f(c, state):
        st_s[c, :, :HEAD_DIM] = state.astype(BF16)
        return state * cdec_f + kv_s[c, :, :HEAD_DIM]

    def scan_b(i, state):
        c = n_ch - 1 - i
        st_s[c, :, HEAD_DIM:] = state.astype(BF16)
        return state * cdec_b + kv_s[c, :, HEAD_DIM:]

    lax.fori_loop(0, n_ch, scan_f, zero_state, unroll=4)
    lax.fori_loop(0, n_ch, scan_b, zero_state, unroll=4)

    def out_step(c, carry):
        rows = chunk_rows(c)
        qb = q_ref[rows, :]
        inner = lax.dot_general(qb, k_ref[rows, :], (((1,), (1,)), ((), ())),
                                preferred_element_type=F32) * intra_s[...]
        o = jnp.dot(inner.astype(BF16), v_ref[rows, :], preferred_element_type=F32)
        cross = jnp.dot(qb, st_s[c], preferred_element_type=F32)
        o = o + cross[:, :HEAD_DIM] * tab[T_QF] + cross[:, HEAD_DIM:] * tab[T_QB]
        o = o * lax.rsqrt(jnp.mean(o * o, axis=-1, keepdims=True) + EPS)
        o = o * rw_ref[...]
        g = g_ref[rows, :].astype(F32)
        o = o * (g * (1.0 / (1.0 + jnp.exp(-g))))
        o_ref[rows, :] = o.astype(o_ref.dtype)
        return carry

    lax.fori_loop(0, n_ch, out_step, 0, unroll=n_ch)


def _retention(decay_f, decay_b, qkvg, ret_norm_w, w_out):
    _, batch, heads, seq, dh = qkvg.shape

    def spec(t):
        return pl.BlockSpec((None, None, None, seq, dh),
                            lambda h, b, t=t: (t, b, h, 0, 0))

    smem = pl.BlockSpec(memory_space=pltpu.SMEM)
    w_specs, w_shapes = _cast_specs([w_out], heads * batch, lambda h, b: h * batch + b)
    return pl.pallas_call(
        _ret_kernel,
        grid=(heads, batch),
        in_specs=[smem, smem, spec(0), spec(1), spec(2), spec(3),
                  pl.BlockSpec((None, 1, dh), lambda h, b: (h, 0, 0))] + w_specs,
        out_specs=[pl.BlockSpec((None, None, seq, dh), lambda h, b: (b, h, 0, 0))] + w_specs,
        out_shape=[jax.ShapeDtypeStruct((batch, heads, seq, dh), BF16)] + w_shapes,
        scratch_shapes=[pltpu.VMEM((RET_BLOCK, RET_BLOCK), F32),
                        pltpu.VMEM((4, RET_BLOCK, dh), F32),
                        pltpu.VMEM((seq // RET_BLOCK, dh, 2 * dh), F32),
                        pltpu.VMEM((seq // RET_BLOCK, dh, 2 * dh), BF16)],
        compiler_params=_params(("parallel", "arbitrary"), 32),
        name="retention",
    )(decay_f, decay_b, qkvg, qkvg, qkvg, qkvg, ret_norm_w, w_out)


def _outproj_kernel(a_ref, r_ref, x_ref, w_ref, nw_ref, h_ref, n_ref):
    mixed = jnp.concatenate([a_ref[h] for h in range(N_HEADS)]
                            + [r_ref[h] for h in range(N_HEADS)], axis=-1)
    hid = x_ref[...] + jnp.dot(mixed, w_ref[...], preferred_element_type=F32)
    h_ref[...] = hid
    n_ref[...] = _rmsnorm_rows(hid, nw_ref[...]).astype(n_ref.dtype)


def _outproj(attn, ret, x2, w_out, norm_w, tm=512):
    batch, heads, seq, dh = attn.shape
    n_tok, d_model = x2.shape
    s_blocks = seq // tm
    head_spec = pl.BlockSpec((None, heads, tm, dh),
                             lambda i: (i // s_blocks, 0, i % s_blocks, 0))
    row_spec = pl.BlockSpec((tm, d_model), lambda i: (i, 0))
    return pl.pallas_call(
        _outproj_kernel,
        grid=(n_tok // tm,),
        in_specs=[head_spec, head_spec, row_spec,
                  pl.BlockSpec(w_out.shape, lambda i: (0, 0)),
                  pl.BlockSpec((1, d_model), lambda i: (0, 0))],
        out_specs=[row_spec, row_spec],
        out_shape=[jax.ShapeDtypeStruct((n_tok, d_model), F32),
                   jax.ShapeDtypeStruct((n_tok, d_model), BF16)],
        compiler_params=_params(("parallel",), 56),
        name="outproj",
    )(attn, ret, x2, w_out, norm_w)


def _ffn_kernel(n_ref, h_ref, wg_ref, wu_ref, wd_ref, fw_ref, o_ref, *,
                final_norm, h_slices):
    f = pl.program_id(1)
    h_rows = h_ref.shape[0]

    @pl.when(f == 0)
    def _():
        o_ref[...] = jnp.zeros_like(o_ref)

    n = n_ref[...]
    g = jnp.dot(n, wg_ref[...], preferred_element_type=F32)
    u = jnp.dot(n, wu_ref[...], preferred_element_type=F32)
    a = (g * (1.0 / (1.0 + jnp.exp(-g))) * u).astype(BF16)
    o_ref[...] += jnp.dot(a, wd_ref[...], preferred_element_type=F32)

    slice_idx = jnp.minimum(f, h_slices - 1)
    rows = pl.ds(pl.multiple_of(slice_idx * h_rows, h_rows), h_rows)
    o_ref[rows, :] += h_ref[...] * (f < h_slices).astype(F32)

    if final_norm:
        @pl.when(f == pl.num_programs(1) - 1)
        def _():
            o_ref[...] = _rmsnorm_rows(o_ref[...], fw_ref[...])


def _ffn(n2, hid, w_gate, w_up, w_down, final_w, final_norm, tm=1024, tf=FFN_TF,
         h_slices=8):
    n_tok, d_model = hid.shape
    f_steps = w_gate.shape[1] // tf
    assert f_steps >= h_slices and tm % h_slices == 0
    row = lambda i, f: (i, 0)
    col_tile = pl.BlockSpec((d_model, tf), lambda i, f: (0, f))
    return pl.pallas_call(
        functools.partial(_ffn_kernel, final_norm=final_norm, h_slices=h_slices),
        grid=(n_tok // tm, f_steps),
        in_specs=[pl.BlockSpec((tm, d_model), row),
                  pl.BlockSpec((tm // h_slices, d_model),
                               lambda i, f: (i * h_slices + jnp.minimum(f, h_slices - 1), 0)),
                  col_tile, col_tile,
                  pl.BlockSpec((tf, d_model), lambda i, f: (f, 0)),
                  pl.BlockSpec((1, d_model), lambda i, f: (0, 0))],
        out_specs=pl.BlockSpec((tm, d_model), row),
        out_shape=jax.ShapeDtypeStruct((n_tok, d_model), F32),
        compiler_params=_params(("parallel", "arbitrary"), 56),
        name="ffn",
    )(n2, hid, w_gate, w_up, w_down, final_w)


def kernel(x, norm_mix_w, w_in, ret_decay_fwd, ret_decay_bwd, ret_norm_w, w_out,
           norm_ffn_w, w_gate, w_up, w_down, norm_final_w):
    batch, seq, d_model = x.shape
    depth = w_in.shape[0]
    assert depth >= 1
    slopes = jnp.exp2(-8.0 * jnp.arange(1, N_HEADS + 1, dtype=F32) / N_HEADS)
    final_w = norm_final_w.reshape(1, d_model)

    hid = x.reshape(batch * seq, d_model)
    for layer in range(depth):
        w_in_l = w_in[layer].astype(BF16)
        nw = norm_mix_w[layer].reshape(1, d_model)
        qkv_a, qkvg_r = _inproj(hid, nw, w_in_l, 3, F32, BF16, batch, seq)
        attn, w_gate_l, w_up_l, w_down_l = _attention(
            slopes, qkv_a, [w_gate[layer], w_up[layer], w_down[layer]])
        ret, w_out_l = _retention(ret_decay_fwd[layer], ret_decay_bwd[layer], qkvg_r,
                                  ret_norm_w[layer].reshape(N_HEADS, 1, HEAD_DIM),
                                  w_out[layer])
        hid, n2 = _outproj(attn, ret, hid, w_out_l, norm_ffn_w[layer].reshape(1, d_model))
        last = layer == depth - 1
        hid = _ffn(n2, hid, w_gate_l, w_up_l, w_down_l, final_w, final_norm=last)
    return hid.reshape(batch, seq, d_model)
```

```python
import functools

import jax
import jax.numpy as jnp
from jax import lax
from jax.experimental import pallas as pl
from jax.experimental.pallas import tpu as pltpu

F32 = jnp.float32
BF16 = jnp.bfloat16

HEAD_DIM = 128
N_HEADS = 8
D_GROUP = N_HEADS * HEAD_DIM
DILATED_PATTERNS = ((128, 1), (512, 4), (2048, 16))
HALF_BAND = 64
RET_BLOCK = 256
EPS = 1e-6
NEG_BIG = -1e30
LOG2_E = 1.4426950408889634
MIB = 1024 * 1024

FFN_TF = 512
ATT_TQ = 128
ATT_TK = ATT_TQ + 2 * HALF_BAND


def _params(semantics, vmem_mib, flags=None):
    return pltpu.CompilerParams(dimension_semantics=semantics,
                                vmem_limit_bytes=vmem_mib * MIB, flags=flags)


def _to_bf16(x, mul=None):
    return (x if mul is None else x * mul).astype(BF16)


def _rmsnorm_rows(x, w):
    return x * lax.rsqrt(jnp.mean(x * x, axis=-1, keepdims=True) + EPS) * w


def _inproj_kernel(x_ref, nw_ref, w_ref, oa_ref, ob_ref, n_scr, *,
                   groups, groups_a, x_slices):
    i = pl.program_id(0)
    j = pl.program_id(1)
    slice_rows = x_ref.shape[0]

    def norm_slice():
        rows = pl.ds(pl.multiple_of(j * slice_rows, slice_rows), slice_rows)
        n_scr[i & 1, rows, :] = _rmsnorm_rows(x_ref[...], nw_ref[...]).astype(BF16)

    def project(o_ref, with_norm):
        res = jnp.dot(n_scr[(i - 1) & 1], w_ref[...], preferred_element_type=F32)
        for h in range(N_HEADS):
            o_ref[h] = res[:, h * HEAD_DIM:(h + 1) * HEAD_DIM].astype(o_ref.dtype)
        if with_norm:
            norm_slice()

    @pl.when(j == 0)
    def _():
        ob_ref[...] = jnp.zeros_like(ob_ref)

    @pl.when((i == 0) & (j == 0))
    def _():
        oa_ref[...] = jnp.zeros_like(oa_ref)

    pl.when((i == 0) & (j < x_slices))(norm_slice)
    cases = {}
    for step in range(groups):
        cases.setdefault((step < groups_a, step < x_slices), []).append(step)
    for (to_a, with_norm), steps in cases.items():
        pl.when((i > 0) & (j >= steps[0]) & (j <= steps[-1]))(
            functools.partial(project, oa_ref if to_a else ob_ref, with_norm))


def _inproj(x2, norm_w, w, groups_a, dtype_a, dtype_b, batch, seq, tm=1024, x_slices=4):
    n_tok, d_model = x2.shape
    groups = w.shape[1] // D_GROUP
    groups_b = groups - groups_a
    n_tiles = n_tok // tm
    s_blocks = seq // tm
    assert x_slices <= groups and tm % x_slices == 0

    def out_spec(group_of_j):
        def index(i, j):
            t = jnp.maximum(i - 1, 0)
            g = jnp.where(i == 0, 0, group_of_j(j))
            return (g, t // s_blocks, 0, t % s_blocks, 0)
        return pl.BlockSpec((None, None, N_HEADS, tm, HEAD_DIM), index)

    def out_shape(n_groups, dtype):
        return jax.ShapeDtypeStruct((n_groups, batch, N_HEADS, seq, HEAD_DIM), dtype)

    return pl.pallas_call(
        functools.partial(_inproj_kernel, groups=groups, groups_a=groups_a,
                          x_slices=x_slices),
        grid=(n_tiles + 1, groups),
        in_specs=[
            pl.BlockSpec((tm // x_slices, d_model),
                         lambda i, j: (jnp.minimum(i, n_tiles - 1) * x_slices
                                       + jnp.minimum(j, x_slices - 1), 0)),
            pl.BlockSpec((1, d_model), lambda i, j: (0, 0)),
            pl.BlockSpec((d_model, D_GROUP), lambda i, j: (0, jnp.where(i == 0, 0, j))),
        ],
        out_specs=[out_spec(lambda j: jnp.minimum(j, groups_a - 1)),
                   out_spec(lambda j: jnp.maximum(j - groups_a, 0))],
        out_shape=[out_shape(groups_a, dtype_a), out_shape(groups_b, dtype_b)],
        scratch_shapes=[pltpu.VMEM((2, tm, d_model), BF16)],
        compiler_params=_params(("arbitrary", "arbitrary"), 48),
        name="inproj",
    )(x2, norm_w, w)


def _attn_kernel(slopes_ref, q_ref, k_ref, v_ref, wg_ref, wu_ref, wd_ref,
                 o_ref, wg_bf, wu_bf, wd_bf,
                 q4f, k4f, v4f,
                 num4, den4, m4, num1, den1, m1, bias_s):
    for w_src, w_dst in ((wg_ref, wg_bf), (wu_ref, wu_bf), (wd_ref, wd_bf)):
        w_dst[...] = w_src[...].astype(BF16)

    seq = q_ref.shape[0]
    quarter = seq // 4
    q_scale = HEAD_DIM ** -0.5 * LOG2_E

    @pl.when(pl.program_id(1) == 0)
    def _():
        slope = slopes_ref[pl.program_id(0)]
        row = lax.broadcasted_iota(jnp.int32, (ATT_TQ, ATT_TK), 0)
        col = lax.broadcasted_iota(jnp.int32, (ATT_TQ, ATT_TK), 1)
        for p, (_, dil) in enumerate(DILATED_PATTERNS):
            for c, off in enumerate((0, HALF_BAND, 2 * HALF_BAND)):
                dist = jnp.abs(row - col + off)
                bias = -slope * (dil * dist).astype(F32) * LOG2_E
                bias_s[p * 3 + c] = jnp.where(dist <= HALF_BAND, bias, NEG_BIG)

    for src, dst in ((q_ref, q4f), (k_ref, k4f), (v_ref, v4f)):
        for r in range(4):
            dst[r * quarter:(r + 1) * quarter, :] = src[pl.ds(r, quarter, stride=4), :]

    def run_pattern(p, dil, sources, rows_of, incoming, emit):
        q_src, k_src, v_src = sources
        sub_len = seq // dil
        n_blk = sub_len // ATT_TQ
        shift = n_blk.bit_length() - 1
        assert n_blk == 1 << shift and n_blk >= 2

        def block(idx, carry):
            r = idx >> shift
            blk = idx & (n_blk - 1)
            q0 = blk * ATT_TQ
            k0 = jnp.clip(q0 - HALF_BAND, 0, sub_len - ATT_TK)
            q_rows = pl.ds(pl.multiple_of(r * sub_len + q0, ATT_TQ), ATT_TQ)
            case = jnp.where(blk == 0, 0, jnp.where(blk == n_blk - 1, 2, 1))
            q = _to_bf16(q_src[rows_of(r, q0, ATT_TQ), :], q_scale)
            k = _to_bf16(k_src[rows_of(r, k0, ATT_TK), :])
            v = _to_bf16(v_src[rows_of(r, k0, ATT_TK), :])
            s = lax.dot_general(q, k, (((1,), (1,)), ((), ())),
                                preferred_element_type=F32)
            s = s + bias_s[p * 3 + case]
            m_blk = jnp.max(s, axis=-1, keepdims=True)
            if incoming is None:
                m_new = jnp.broadcast_to(m_blk, (ATT_TQ, HEAD_DIM))
            else:
                num_in, den_in, m_in = incoming
                m_old = m_in[q_rows, :]
                m_new = jnp.maximum(m_old, m_blk)
            e = jnp.exp2(s - jnp.concatenate([m_new, m_new], axis=-1))
            pv = jnp.dot(e.astype(BF16), jnp.concatenate([v, jnp.ones_like(v)], axis=-1),
                         preferred_element_type=F32)
            num = pv[:, :HEAD_DIM]
            den = pv[:, HEAD_DIM:]
            if incoming is not None:
                alpha = jnp.exp2(m_old - m_new)
                num = alpha * num_in[q_rows, :] + num
                den = alpha * den_in[q_rows, :] + den
            emit(r, q0, num, den, m_new)
            return carry

        lax.fori_loop(0, dil * n_blk, block, 0, unroll=32)

    copies4 = (q4f, k4f, v4f)

    def rows16(r, l0, n):
        return pl.ds((r & 3) * quarter + 4 * l0 + (r >> 2), n, stride=4)

    def emit16(r, q0, num, den, m_new):
        rows = pl.ds((r & 3) * quarter + 4 * q0 + (r >> 2), ATT_TQ, stride=4)
        num4[rows, :] = num
        den4[rows, :] = den
        m4[rows, :] = m_new

    run_pattern(2, 16, copies4, rows16, None, emit16)

    def rows4(r, l0, n):
        return pl.ds(pl.multiple_of(r * quarter + l0, HALF_BAND), n)

    def emit4(r, q0, num, den, m_new):
        rows = pl.ds(4 * q0 + r, ATT_TQ, stride=4)
        num1[rows, :] = num
        den1[rows, :] = den
        m1[rows, :] = m_new

    run_pattern(1, 4, copies4, rows4, (num4, den4, m4), emit4)

    def rows1(r, l0, n):
        return pl.ds(pl.multiple_of(l0, HALF_BAND), n)

    def emit1(r, q0, num, den, m_new):
        o_ref[pl.ds(pl.multiple_of(q0, ATT_TQ), ATT_TQ), :] = (num / den).astype(o_ref.dtype)

    run_pattern(0, 1, (q_ref, k_ref, v_ref), rows1, (num1, den1, m1), emit1)


def _cast_specs(weights, n_steps, step_of):
    specs, shapes = [], []
    for w in weights:
        rows = w.shape[0] // n_steps
        assert rows * n_steps == w.shape[0] and rows % 16 == 0
        specs.append(pl.BlockSpec((rows, w.shape[1]), lambda *idx: (step_of(*idx), 0)))
        shapes.append(jax.ShapeDtypeStruct(w.shape, BF16))
    return specs, shapes


def _attention(slopes, qkv, ffn_weights):
    _, batch, heads, seq, dh = qkv.shape
    assert seq % (16 * ATT_TK) == 0 and dh == HEAD_DIM

    def spec(t):
        return pl.BlockSpec((None, None, None, seq, dh),
                            lambda h, b, t=t: (t, b, h, 0, 0))

    w_specs, w_shapes = _cast_specs(ffn_weights, heads * batch, lambda h, b: h * batch + b)
    f32_rows = pltpu.VMEM((seq, dh), F32)
    return pl.pallas_call(
        _attn_kernel,
        grid=(heads, batch),
        in_specs=[pl.BlockSpec(memory_space=pltpu.SMEM), spec(0), spec(1), spec(2)] + w_specs,
        out_specs=[pl.BlockSpec((None, None, seq, dh), lambda h, b: (b, h, 0, 0))] + w_specs,
        out_shape=[jax.ShapeDtypeStruct((batch, heads, seq, dh), BF16)] + w_shapes,
        scratch_shapes=[
            f32_rows, f32_rows, f32_rows,
            f32_rows, f32_rows, f32_rows, f32_rows, f32_rows, f32_rows,
            pltpu.VMEM((3 * len(DILATED_PATTERNS), ATT_TQ, ATT_TK), F32),
        ],
        compiler_params=_params(("parallel", "arbitrary"), 56),
        name="dilated_attention",
    )(slopes, qkv, qkv, qkv, *ffn_weights)


def _ret_kernel(df_ref, db_ref, q_ref, k_ref, v_ref, g_ref, rw_ref, wo_ref,
                o_ref, wo_bf, intra_s, tab, kv_s, st_s):
    wo_bf[...] = wo_ref[...].astype(BF16)

    seq = q_ref.shape[0]
    ch = RET_BLOCK
    n_ch = seq // ch
    h = pl.program_id(0)
    scale = HEAD_DIM ** -0.5
    T_QF, T_QB, T_KF, T_KB = range(4)

    lg_f = -jnp.exp(jnp.full((1, HEAD_DIM), df_ref[h], F32))
    lg_b = -jnp.exp(jnp.full((1, HEAD_DIM), db_ref[h], F32))

    @pl.when(pl.program_id(1) == 0)
    def _():
        rel = (lax.broadcasted_iota(jnp.int32, (ch, ch), 0)
               - lax.broadcasted_iota(jnp.int32, (ch, ch), 1)).astype(F32)
        intra_s[...] = jnp.where(rel >= 0.0,
                                 jnp.exp(lg_f[:, :1] * jnp.maximum(rel, 0.0)),
                                 jnp.exp(lg_b[:, :1] * jnp.maximum(-rel, 0.0))) * scale
        ri = lax.broadcasted_iota(jnp.int32, (ch, HEAD_DIM), 0).astype(F32)
        tab[T_QF] = jnp.exp(lg_f * (ri + 1.0)) * scale
        tab[T_QB] = jnp.exp(lg_b * (ch - ri)) * scale
        tab[T_KF] = jnp.exp(lg_f * (ch - 1.0 - ri))
        tab[T_KB] = jnp.exp(lg_b * ri)

    cdec_f = jnp.exp(lg_f * ch)
    cdec_b = jnp.exp(lg_b * ch)

    def chunk_rows(c):
        return pl.ds(pl.multiple_of(c * ch, ch), ch)

    def kv_step(c, carry):
        rows = chunk_rows(c)
        vc = v_ref[rows, :].astype(F32)
        v2 = jnp.concatenate([vc * tab[T_KF], vc * tab[T_KB]], axis=-1).astype(BF16)
        kv_s[c] = lax.dot_general(k_ref[rows, :], v2, (((0,), (0,)), ((), ())),
                                  preferred_element_type=F32)
        return carry

    lax.fori_loop(0, n_ch, kv_step, 0, unroll=8)
    zero_state = jnp.zeros((HEAD_DIM, HEAD_DIM), F32)

    def scan_f(c, state):
        st_s[c, :, :HEAD_DIM] = state.astype(BF16)
        return state * cdec_f + kv_s[c, :, :HEAD_DIM]

    def scan_b(i, state):
        c = n_ch - 1 - i
        st_s[c, :, HEAD_DIM:] = state.astype(BF16)
        return state * cdec_b + kv_s[c, :, HEAD_DIM:]

    lax.fori_loop(0, n_ch, scan_f, zero_state, unroll=4)
    lax.fori_loop(0, n_ch, scan_b, zero_state, unroll=4)

    def out_step(c, carry):
        rows = chunk_rows(c)
        qb = q_ref[rows, :]
        inner = lax.dot_general(qb, k_ref[rows, :], (((1,), (1,)), ((), ())),
                                preferred_element_type=F32) * intra_s[...]
        o = jnp.dot(inner.astype(BF16), v_ref[rows, :], preferred_element_type=F32)
        cross = jnp.dot(qb, st_s[c], preferred_element_type=F32)
        o = o + cross[:, :HEAD_DIM] * tab[T_QF] + cross[:, HEAD_DIM:] * tab[T_QB]
        o = o * lax.rsqrt(jnp.mean(o * o, axis=-1, keepdims=True) + EPS)
        o = o * rw_ref[...]
        g = g_ref[rows, :].astype(F32)
        o = o * (g * (1.0 / (1.0 + jnp.exp(-g))))
        o_ref[rows, :] = o.astype(o_ref.dtype)
        return carry

    lax.fori_loop(0, n_ch, out_step, 0, unroll=n_ch)


def _retention(decay_f, decay_b, qkvg, ret_norm_w, w_out):
    _, batch, heads, seq, dh = qkvg.shape

    def spec(t):
        return pl.BlockSpec((None, None, None, seq, dh),
                            lambda h, b, t=t: (t, b, h, 0, 0))

    smem = pl.BlockSpec(memory_space=pltpu.SMEM)
    w_specs, w_shapes = _cast_specs([w_out], heads * batch, lambda h, b: h * batch + b)
    return pl.pallas_call(
        _ret_kernel,
        grid=(heads, batch),
        in_specs=[smem, smem, spec(0), spec(1), spec(2), spec(3),
                  pl.BlockSpec((None, 1, dh), lambda h, b: (h, 0, 0))] + w_specs,
        out_specs=[pl.BlockSpec((None, None, seq, dh), lambda h, b: (b, h, 0, 0))] + w_specs,
        out_shape=[jax.ShapeDtypeStruct((batch, heads, seq, dh), BF16)] + w_shapes,
        scratch_shapes=[pltpu.VMEM((RET_BLOCK, RET_BLOCK), F32),
                        pltpu.VMEM((4, RET_BLOCK, dh), F32),
                        pltpu.VMEM((seq // RET_BLOCK, dh, 2 * dh), F32),
                        pltpu.VMEM((seq // RET_BLOCK, dh, 2 * dh), BF16)],
        compiler_params=_params(("parallel", "arbitrary"), 32),
        name="retention",
    )(decay_f, decay_b, qkvg, qkvg, qkvg, qkvg, ret_norm_w, w_out)


def _outproj_kernel(a_ref, r_ref, x_ref, w_ref, nw_ref, h_ref, n_ref):
    mixed = jnp.concatenate([a_ref[h] for h in range(N_HEADS)]
                            + [r_ref[h] for h in range(N_HEADS)], axis=-1)
    hid = x_ref[...] + jnp.dot(mixed, w_ref[...], preferred_element_type=F32)
    h_ref[...] = hid
    n_ref[...] = _rmsnorm_rows(hid, nw_ref[...]).astype(n_ref.dtype)


def _outproj(attn, ret, x2, w_out, norm_w, tm=512):
    batch, heads, seq, dh = attn.shape
    n_tok, d_model = x2.shape
    s_blocks = seq // tm
    head_spec = pl.BlockSpec((None, heads, tm, dh),
                             lambda i: (i // s_blocks, 0, i % s_blocks, 0))
    row_spec = pl.BlockSpec((tm, d_model), lambda i: (i, 0))
    return pl.pallas_call(
        _outproj_kernel,
        grid=(n_tok // tm,),
        in_specs=[head_spec, head_spec, row_spec,
                  pl.BlockSpec(w_out.shape, lambda i: (0, 0)),
                  pl.BlockSpec((1, d_model), lambda i: (0, 0))],
        out_specs=[row_spec, row_spec],
        out_shape=[jax.ShapeDtypeStruct((n_tok, d_model), F32),
                   jax.ShapeDtypeStruct((n_tok, d_model), BF16)],
        compiler_params=_params(("parallel",), 56),
        name="outproj",
    )(attn, ret, x2, w_out, norm_w)


def _ffn_kernel(n_ref, h_ref, wg_ref, wu_ref, wd_ref, fw_ref, o_ref, *,
                final_norm, h_slices):
    f = pl.program_id(1)
    h_rows = h_ref.shape[0]

    @pl.when(f == 0)
    def _():
        o_ref[...] = jnp.zeros_like(o_ref)

    n = n_ref[...]
    g = jnp.dot(n, wg_ref[...], preferred_element_type=F32)
    u = jnp.dot(n, wu_ref[...], preferred_element_type=F32)
    a = (g * (1.0 / (1.0 + jnp.exp(-g))) * u).astype(BF16)
    o_ref[...] += jnp.dot(a, wd_ref[...], preferred_element_type=F32)

    slice_idx = jnp.minimum(f, h_slices - 1)
    rows = pl.ds(pl.multiple_of(slice_idx * h_rows, h_rows), h_rows)
    o_ref[rows, :] += h_ref[...] * (f < h_slices).astype(F32)

    if final_norm:
        @pl.when(f == pl.num_programs(1) - 1)
        def _():
            o_ref[...] = _rmsnorm_rows(o_ref[...], fw_ref[...])


def _ffn(n2, hid, w_gate, w_up, w_down, final_w, final_norm, tm=1024, tf=FFN_TF,
         h_slices=8):
    n_tok, d_model = hid.shape
    f_steps = w_gate.shape[1] // tf
    assert f_steps >= h_slices and tm % h_slices == 0
    row = lambda i, f: (i, 0)
    col_tile = pl.BlockSpec((d_model, tf), lambda i, f: (0, f))
    return pl.pallas_call(
        functools.partial(_ffn_kernel, final_norm=final_norm, h_slices=h_slices),
        grid=(n_tok // tm, f_steps),
        in_specs=[pl.BlockSpec((tm, d_model), row),
                  pl.BlockSpec((tm // h_slices, d_model),
                               lambda i, f: (i * h_slices + jnp.minimum(f, h_slices - 1), 0)),
                  col_tile, col_tile,
                  pl.BlockSpec((tf, d_model), lambda i, f: (f, 0)),
                  pl.BlockSpec((1, d_model), lambda i, f: (0, 0))],
        out_specs=pl.BlockSpec((tm, d_model), row),
        out_shape=jax.ShapeDtypeStruct((n_tok, d_model), F32),
        compiler_params=_params(("parallel", "arbitrary"), 56),
        name="ffn",
    )(n2, hid, w_gate, w_up, w_down, final_w)


def kernel(x, norm_mix_w, w_in, ret_decay_fwd, ret_decay_bwd, ret_norm_w, w_out,
           norm_ffn_w, w_gate, w_up, w_down, norm_final_w):
    batch, seq, d_model = x.shape
    depth = w_in.shape[0]
    assert depth >= 1
    slopes = jnp.exp2(-8.0 * jnp.arange(1, N_HEADS + 1, dtype=F32) / N_HEADS)
    final_w = norm_final_w.reshape(1, d_model)

    hid = x.reshape(batch * seq, d_model)
    for layer in range(depth):
        w_in_l = w_in[layer].astype(BF16)
        nw = norm_mix_w[layer].reshape(1, d_model)
        qkv_a, qkvg_r = _inproj(hid, nw, w_in_l, 3, F32, BF16, batch, seq)
        attn, w_gate_l, w_up_l, w_down_l = _attention(
            slopes, qkv_a, [w_gate[layer], w_up[layer], w_down[layer]])
        ret, w_out_l = _retention(ret_decay_fwd[layer], ret_decay_bwd[layer], qkvg_r,
                                  ret_norm_w[layer].reshape(N_HEADS, 1, HEAD_DIM),
                                  w_out[layer])
        hid, n2 = _outproj(attn, ret, hid, w_out_l, norm_ffn_w[layer].reshape(1, d_model))
        last = layer == depth - 1
        hid = _ffn(n2, hid, w_gate_l, w_up_l, w_down_l, final_w, final_norm=last)
    return hid.reshape(batch, seq, d_model)
```

```python
import functools

import jax
import jax.numpy as jnp
from jax import lax
from jax.experimental import pallas as pl
from jax.experimental.pallas import tpu as pltpu

F32 = jnp.float32
BF16 = jnp.bfloat16

HEAD_DIM = 128
N_HEADS = 8
D_GROUP = N_HEADS * HEAD_DIM
DILATED_PATTERNS = ((128, 1), (512, 4), (2048, 16))
HALF_BAND = 64
RET_BLOCK = 256
EPS = 1e-6
NEG_BIG = -1e30
LOG2_E = 1.4426950408889634
MIB = 1024 * 1024

FFN_TF = 512
ATT_TQ = 128
ATT_TK = ATT_TQ + 2 * HALF_BAND


def _params(semantics, vmem_mib, flags=None):
    return pltpu.CompilerParams(dimension_semantics=semantics,
                                vmem_limit_bytes=vmem_mib * MIB, flags=flags)


def _to_bf16(x, mul=None):
    return (x if mul is None else x * mul).astype(BF16)


def _rmsnorm_rows(x, w):
    return x * lax.rsqrt(jnp.mean(x * x, axis=-1, keepdims=True) + EPS) * w


def _inproj_kernel(x_ref, nw_ref, w_ref, oa_ref, ob_ref, n_scr, *,
                   groups, groups_a, x_slices):
    i = pl.program_id(0)
    j = pl.program_id(1)
    slice_rows = x_ref.shape[0]

    def norm_slice():
        rows = pl.ds(pl.multiple_of(j * slice_rows, slice_rows), slice_rows)
        n_scr[i & 1, rows, :] = _rmsnorm_rows(x_ref[...], nw_ref[...]).astype(BF16)

    def project(o_ref, with_norm):
        res = jnp.dot(n_scr[(i - 1) & 1], w_ref[...], preferred_element_type=F32)
        for h in range(N_HEADS):
            o_ref[h] = res[:, h * HEAD_DIM:(h + 1) * HEAD_DIM].astype(o_ref.dtype)
        if with_norm:
            norm_slice()

    @pl.when(j == 0)
    def _():
        ob_ref[...] = jnp.zeros_like(ob_ref)

    @pl.when((i == 0) & (j == 0))
    def _():
        oa_ref[...] = jnp.zeros_like(oa_ref)

    pl.when((i == 0) & (j < x_slices))(norm_slice)
    cases = {}
    for step in range(groups):
        cases.setdefault((step < groups_a, step < x_slices), []).append(step)
    for (to_a, with_norm), steps in cases.items():
        pl.when((i > 0) & (j >= steps[0]) & (j <= steps[-1]))(
            functools.partial(project, oa_ref if to_a else ob_ref, with_norm))


def _inproj(x2, norm_w, w, groups_a, dtype_a, dtype_b, batch, seq, tm=1024, x_slices=4):
    n_tok, d_model = x2.shape
    groups = w.shape[1] // D_GROUP
    groups_b = groups - groups_a
    n_tiles = n_tok // tm
    s_blocks = seq // tm
    assert x_slices <= groups and tm % x_slices == 0

    def out_spec(group_of_j):
        def index(i, j):
            t = jnp.maximum(i - 1, 0)
            g = jnp.where(i == 0, 0, group_of_j(j))
            return (g, t // s_blocks, 0, t % s_blocks, 0)
        return pl.BlockSpec((None, None, N_HEADS, tm, HEAD_DIM), index)

    def out_shape(n_groups, dtype):
        return jax.ShapeDtypeStruct((n_groups, batch, N_HEADS, seq, HEAD_DIM), dtype)

    return pl.pallas_call(
        functools.partial(_inproj_kernel, groups=groups, groups_a=groups_a,
                          x_slices=x_slices),
        grid=(n_tiles + 1, groups),
        in_specs=[
            pl.BlockSpec((tm // x_slices, d_model),
                         lambda i, j: (jnp.minimum(i, n_tiles - 1) * x_slices
                                       + jnp.minimum(j, x_slices - 1), 0)),
            pl.BlockSpec((1, d_model), lambda i, j: (0, 0)),
            pl.BlockSpec((d_model, D_GROUP), lambda i, j: (0, jnp.where(i == 0, 0, j))),
        ],
        out_specs=[out_spec(lambda j: jnp.minimum(j, groups_a - 1)),
                   out_spec(lambda j: jnp.maximum(j - groups_a, 0))],
        out_shape=[out_shape(groups_a, dtype_a), out_shape(groups_b, dtype_b)],
        scratch_shapes=[pltpu.VMEM((2, tm, d_model), BF16)],
        compiler_params=_params(("arbitrary", "arbitrary"), 48),
        name="inproj",
    )(x2, norm_w, w)


def _attn_kernel(slopes_ref, q_ref, k_ref, v_ref, wg_ref, wu_ref, wd_ref,
                 o_ref, wgu_bf, wd_bf,
                 q4f, k4f, v4f,
                 num4, den4, m4, num1, den1, m1, bias_s):
    wd_bf[...] = wd_ref[...].astype(BF16)
    for f in range(wg_ref.shape[1] // FFN_TF):
        cols = slice(f * FFN_TF, (f + 1) * FFN_TF)
        wgu_bf[:, 2 * f * FFN_TF:(2 * f + 1) * FFN_TF] = wg_ref[:, cols].astype(BF16)
        wgu_bf[:, (2 * f + 1) * FFN_TF:(2 * f + 2) * FFN_TF] = wu_ref[:, cols].astype(BF16)

    seq = q_ref.shape[0]
    quarter = seq // 4
    q_scale = HEAD_DIM ** -0.5 * LOG2_E

    @pl.when(pl.program_id(1) == 0)
    def _():
        slope = slopes_ref[pl.program_id(0)]
        row = lax.broadcasted_iota(jnp.int32, (ATT_TQ, ATT_TK), 0)
        col = lax.broadcasted_iota(jnp.int32, (ATT_TQ, ATT_TK), 1)
        for p, (_, dil) in enumerate(DILATED_PATTERNS):
            for c, off in enumerate((0, HALF_BAND, 2 * HALF_BAND)):
                dist = jnp.abs(row - col + off)
                bias = -slope * (dil * dist).astype(F32) * LOG2_E
                bias_s[p * 3 + c] = jnp.where(dist <= HALF_BAND, bias, NEG_BIG)

    for src, dst in ((q_ref, q4f), (k_ref, k4f), (v_ref, v4f)):
        for r in range(4):
            dst[r * quarter:(r + 1) * quarter, :] = src[pl.ds(r, quarter, stride=4), :]

    def run_pattern(p, dil, sources, rows_of, incoming, emit):
        q_src, k_src, v_src = sources
        sub_len = seq // dil
        n_blk = sub_len // ATT_TQ
        shift = n_blk.bit_length() - 1
        assert n_blk == 1 << shift and n_blk >= 2

        def block(idx, carry):
            r = idx >> shift
            blk = idx & (n_blk - 1)
            q0 = blk * ATT_TQ
            k0 = jnp.clip(q0 - HALF_BAND, 0, sub_len - ATT_TK)
            q_rows = pl.ds(pl.multiple_of(r * sub_len + q0, ATT_TQ), ATT_TQ)
            case = jnp.where(blk == 0, 0, jnp.where(blk == n_blk - 1, 2, 1))
            q = _to_bf16(q_src[rows_of(r, q0, ATT_TQ), :], q_scale)
            k = _to_bf16(k_src[rows_of(r, k0, ATT_TK), :])
            v = _to_bf16(v_src[rows_of(r, k0, ATT_TK), :])
            s = lax.dot_general(q, k, (((1,), (1,)), ((), ())),
                                preferred_element_type=F32)
            s = s + bias_s[p * 3 + case]
            m_blk = jnp.max(s, axis=-1, keepdims=True)
            if incoming is None:
                m_new = jnp.broadcast_to(m_blk, (ATT_TQ, HEAD_DIM))
            else:
                num_in, den_in, m_in = incoming
                m_old = m_in[q_rows, :]
                m_new = jnp.maximum(m_old, m_blk)
            e = jnp.exp2(s - jnp.concatenate([m_new, m_new], axis=-1))
            pv = jnp.dot(e.astype(BF16), jnp.concatenate([v, jnp.ones_like(v)], axis=-1),
                         preferred_element_type=F32)
            num = pv[:, :HEAD_DIM]
            den = pv[:, HEAD_DIM:]
            if incoming is not None:
                alpha = jnp.exp2(m_old - m_new)
                num = alpha * num_in[q_rows, :] + num
                den = alpha * den_in[q_rows, :] + den
            emit(r, q0, num, den, m_new)
            return carry

        lax.fori_loop(0, dil * n_blk, block, 0, unroll=32)

    copies4 = (q4f, k4f, v4f)

    def rows16(r, l0, n):
        return pl.ds((r & 3) * quarter + 4 * l0 + (r >> 2), n, stride=4)

    def emit16(r, q0, num, den, m_new):
        rows = pl.ds((r & 3) * quarter + 4 * q0 + (r >> 2), ATT_TQ, stride=4)
        num4[rows, :] = num
        den4[rows, :] = den
        m4[rows, :] = m_new

    run_pattern(2, 16, copies4, rows16, None, emit16)

    def rows4(r, l0, n):
        return pl.ds(pl.multiple_of(r * quarter + l0, HALF_BAND), n)

    def emit4(r, q0, num, den, m_new):
        rows = pl.ds(4 * q0 + r, ATT_TQ, stride=4)
        num1[rows, :] = num
        den1[rows, :] = den
        m1[rows, :] = m_new

    run_pattern(1, 4, copies4, rows4, (num4, den4, m4), emit4)

    def rows1(r, l0, n):
        return pl.ds(pl.multiple_of(l0, HALF_BAND), n)

    def emit1(r, q0, num, den, m_new):
        o_ref[pl.ds(pl.multiple_of(q0, ATT_TQ), ATT_TQ), :] = (num / den).astype(o_ref.dtype)

    run_pattern(0, 1, (q_ref, k_ref, v_ref), rows1, (num1, den1, m1), emit1)


def _cast_specs(weights, n_steps, step_of):
    specs, shapes = [], []
    for w in weights:
        rows = w.shape[0] // n_steps
        assert rows * n_steps == w.shape[0] and rows % 16 == 0
        specs.append(pl.BlockSpec((rows, w.shape[1]), lambda *idx: (step_of(*idx), 0)))
        shapes.append(jax.ShapeDtypeStruct(w.shape, BF16))
    return specs, shapes


def _attention(slopes, qkv, w_gate, w_up, w_down):
    _, batch, heads, seq, dh = qkv.shape
    assert seq % (16 * ATT_TK) == 0 and dh == HEAD_DIM

    def spec(t):
        return pl.BlockSpec((None, None, None, seq, dh),
                            lambda h, b, t=t: (t, b, h, 0, 0))

    assert w_gate.shape == w_up.shape and w_gate.shape[1] % FFN_TF == 0
    step_of = lambda h, b: h * batch + b
    w_specs, _ = _cast_specs([w_gate, w_up, w_down], heads * batch, step_of)
    wgu_like = jax.ShapeDtypeStruct((w_gate.shape[0], 2 * w_gate.shape[1]), F32)
    out_w_specs, out_w_shapes = _cast_specs([wgu_like, w_down], heads * batch, step_of)
    f32_rows = pltpu.VMEM((seq, dh), F32)
    return pl.pallas_call(
        _attn_kernel,
        grid=(heads, batch),
        in_specs=[pl.BlockSpec(memory_space=pltpu.SMEM), spec(0), spec(1), spec(2)] + w_specs,
        out_specs=[pl.BlockSpec((None, None, seq, dh), lambda h, b: (b, h, 0, 0))]
        + out_w_specs,
        out_shape=[jax.ShapeDtypeStruct((batch, heads, seq, dh), BF16)] + out_w_shapes,
        scratch_shapes=[
            f32_rows, f32_rows, f32_rows,
            f32_rows, f32_rows, f32_rows, f32_rows, f32_rows, f32_rows,
            pltpu.VMEM((3 * len(DILATED_PATTERNS), ATT_TQ, ATT_TK), F32),
        ],
        compiler_params=_params(("parallel", "arbitrary"), 56),
        name="dilated_attention",
    )(slopes, qkv, qkv, qkv, w_gate, w_up, w_down)


def _ret_kernel(df_ref, db_ref, q_ref, k_ref, v_ref, g_ref, rw_ref, wo_ref,
                o_ref, wo_bf, intra_s, tab, kv_s, st_s):
    wo_bf[...] = wo_ref[...].astype(BF16)

    seq = q_ref.shape[0]
    ch = RET_BLOCK
    n_ch = seq // ch
    h = pl.program_id(0)
    scale = HEAD_DIM ** -0.5
    T_QF, T_QB, T_KF, T_KB = range(4)

    lg_f = -jnp.exp(jnp.full((1, HEAD_DIM), df_ref[h], F32))
    lg_b = -jnp.exp(jnp.full((1, HEAD_DIM), db_ref[h], F32))

    @pl.when(pl.program_id(1) == 0)
    def _():
        rel = (lax.broadcasted_iota(jnp.int32, (ch, ch), 0)
               - lax.broadcasted_iota(jnp.int32, (ch, ch), 1)).astype(F32)
        intra_s[...] = jnp.where(rel >= 0.0,
                                 jnp.exp(lg_f[:, :1] * jnp.maximum(rel, 0.0)),
                                 jnp.exp(lg_b[:, :1] * jnp.maximum(-rel, 0.0))) * scale
        ri = lax.broadcasted_iota(jnp.int32, (ch, HEAD_DIM), 0).astype(F32)
        tab[T_QF] = jnp.exp(lg_f * (ri + 1.0)) * scale
        tab[T_QB] = jnp.exp(lg_b * (ch - ri)) * scale
        tab[T_KF] = jnp.exp(lg_f * (ch - 1.0 - ri))
        tab[T_KB] = jnp.exp(lg_b * ri)

    cdec_f = jnp.exp(lg_f * ch)
    cdec_b = jnp.exp(lg_b * ch)

    def chunk_rows(c):
        return pl.ds(pl.multiple_of(c * ch, ch), ch)

    def kv_step(c, carry):
        rows = chunk_rows(c)
        vc = v_ref[rows, :].astype(F32)
        v2 = jnp.concatenate([vc * tab[T_KF], vc * tab[T_KB]], axis=-1).astype(BF16)
        kv_s[c] = lax.dot_general(k_ref[rows, :], v2, (((0,), (0,)), ((), ())),
                                  preferred_element_type=F32)
        return carry

    lax.fori_loop(0, n_ch, kv_step, 0, unroll=n_ch)
    zero_state = jnp.zeros((HEAD_DIM, HEAD_DIM), F32)

    def scan_f(c, state):
        st_s[c, :, :HEAD_DIM] = state.astype(BF16)
        return state * cdec_f + kv_s[c, :, :HEAD_DIM]

    def scan_b(i, state):
        c = n_ch - 1 - i
        st_s[c, :, HEAD_DIM:] = state.astype(BF16)
        return state * cdec_b + kv_s[c, :, HEAD_DIM:]

    lax.fori_loop(0, n_ch, scan_f, zero_state, unroll=n_ch)
    lax.fori_loop(0, n_ch, scan_b, zero_state, unroll=n_ch)

    def out_step(c, carry):
        rows = chunk_rows(c)
        qb = q_ref[rows, :]
        inner = lax.dot_general(qb, k_ref[rows, :], (((1,), (1,)), ((), ())),
                                preferred_element_type=F32) * intra_s[...]
        o = jnp.dot(inner.astype(BF16), v_ref[rows, :], preferred_element_type=F32)
        cross = jnp.dot(qb, st_s[c], preferred_element_type=F32)
        o = o + cross[:, :HEAD_DIM] * tab[T_QF] + cross[:, HEAD_DIM:] * tab[T_QB]
        o = o * lax.rsqrt(jnp.mean(o * o, axis=-1, keepdims=True) + EPS)
        o = o * rw_ref[...]
        g = g_ref[rows, :].astype(F32)
        o = o * (g * (1.0 / (1.0 + jnp.exp(-g))))
        o_ref[rows, :] = o.astype(o_ref.dtype)
        return carry

    lax.fori_loop(0, n_ch, out_step, 0, unroll=n_ch)


def _retention(decay_f, decay_b, qkvg, ret_norm_w, w_out):
    _, batch, heads, seq, dh = qkvg.shape

    def spec(t):
        return pl.BlockSpec((None, None, None, seq, dh),
                            lambda h, b, t=t: (t, b, h, 0, 0))

    smem = pl.BlockSpec(memory_space=pltpu.SMEM)
    w_specs, w_shapes = _cast_specs([w_out], heads * batch, lambda h, b: h * batch + b)
    return pl.pallas_call(
        _ret_kernel,
        grid=(heads, batch),
        in_specs=[smem, smem, spec(0), spec(1), spec(2), spec(3),
                  pl.BlockSpec((None, 1, dh), lambda h, b: (h, 0, 0))] + w_specs,
        out_specs=[pl.BlockSpec((None, None, seq, dh), lambda h, b: (b, h, 0, 0))] + w_specs,
        out_shape=[jax.ShapeDtypeStruct((batch, heads, seq, dh), BF16)] + w_shapes,
        scratch_shapes=[pltpu.VMEM((RET_BLOCK, RET_BLOCK), F32),
                        pltpu.VMEM((4, RET_BLOCK, dh), F32),
                        pltpu.VMEM((seq // RET_BLOCK, dh, 2 * dh), F32),
                        pltpu.VMEM((seq // RET_BLOCK, dh, 2 * dh), BF16)],
        compiler_params=_params(("parallel", "arbitrary"), 32),
        name="retention",
    )(decay_f, decay_b, qkvg, qkvg, qkvg, qkvg, ret_norm_w, w_out)


def _outproj_kernel(a_ref, r_ref, x_ref, w_ref, nw_ref, h_ref, n_ref):
    mixed = jnp.concatenate([a_ref[h] for h in range(N_HEADS)]
                            + [r_ref[h] for h in range(N_HEADS)], axis=-1)
    hid = x_ref[...] + jnp.dot(mixed, w_ref[...], preferred_element_type=F32)
    h_ref[...] = hid
    n_ref[...] = _rmsnorm_rows(hid, nw_ref[...]).astype(n_ref.dtype)


def _outproj(attn, ret, x2, w_out, norm_w, tm=512):
    batch, heads, seq, dh = attn.shape
    n_tok, d_model = x2.shape
    s_blocks = seq // tm
    head_spec = pl.BlockSpec((None, heads, tm, dh),
                             lambda i: (i // s_blocks, 0, i % s_blocks, 0))
    row_spec = pl.BlockSpec((tm, d_model), lambda i: (i, 0))
    return pl.pallas_call(
        _outproj_kernel,
        grid=(n_tok // tm,),
        in_specs=[head_spec, head_spec, row_spec,
                  pl.BlockSpec(w_out.shape, lambda i: (0, 0)),
                  pl.BlockSpec((1, d_model), lambda i: (0, 0))],
        out_specs=[row_spec, row_spec],
        out_shape=[jax.ShapeDtypeStruct((n_tok, d_model), F32),
                   jax.ShapeDtypeStruct((n_tok, d_model), BF16)],
        compiler_params=_params(("parallel",), 56),
        name="outproj",
    )(attn, ret, x2, w_out, norm_w)


def _ffn_kernel(n_ref, h_ref, wgu_ref, wd_ref, fw_ref, o_ref, *,
                final_norm, h_slices):
    f = pl.program_id(1)
    h_rows = h_ref.shape[0]

    @pl.when(f == 0)
    def _():
        o_ref[...] = jnp.zeros_like(o_ref)

    tf = wd_ref.shape[0]
    gu = jnp.dot(n_ref[...], wgu_ref[...], preferred_element_type=F32)
    g = gu[:, :tf]
    u = gu[:, tf:]
    a = (g * (1.0 / (1.0 + jnp.exp(-g))) * u).astype(BF16)
    o_ref[...] += jnp.dot(a, wd_ref[...], preferred_element_type=F32)

    slice_idx = jnp.minimum(f, h_slices - 1)
    rows = pl.ds(pl.multiple_of(slice_idx * h_rows, h_rows), h_rows)
    o_ref[rows, :] += h_ref[...] * (f < h_slices).astype(F32)

    if final_norm:
        @pl.when(f == pl.num_programs(1) - 1)
        def _():
            o_ref[...] = _rmsnorm_rows(o_ref[...], fw_ref[...])


def _ffn(n2, hid, w_gate_up, w_down, final_w, final_norm, tm=1024, tf=FFN_TF,
         h_slices=8):
    n_tok, d_model = hid.shape
    f_steps = w_down.shape[0] // tf
    assert f_steps >= h_slices and tm % h_slices == 0
    row = lambda i, f: (i, 0)
    return pl.pallas_call(
        functools.partial(_ffn_kernel, final_norm=final_norm, h_slices=h_slices),
        grid=(n_tok // tm, f_steps),
        in_specs=[pl.BlockSpec((tm, d_model), row),
                  pl.BlockSpec((tm // h_slices, d_model),
                               lambda i, f: (i * h_slices + jnp.minimum(f, h_slices - 1), 0)),
                  pl.BlockSpec((d_model, 2 * tf), lambda i, f: (0, f)),
                  pl.BlockSpec((tf, d_model), lambda i, f: (f, 0)),
                  pl.BlockSpec((1, d_model), lambda i, f: (0, 0))],
        out_specs=pl.BlockSpec((tm, d_model), row),
        out_shape=jax.ShapeDtypeStruct((n_tok, d_model), F32),
        compiler_params=_params(("parallel", "arbitrary"), 56),
        name="ffn",
    )(n2, hid, w_gate_up, w_down, final_w)


def kernel(x, norm_mix_w, w_in, ret_decay_fwd, ret_decay_bwd, ret_norm_w, w_out,
           norm_ffn_w, w_gate, w_up, w_down, norm_final_w):
    batch, seq, d_model = x.shape
    depth = w_in.shape[0]
    assert depth >= 1
    slopes = jnp.exp2(-8.0 * jnp.arange(1, N_HEADS + 1, dtype=F32) / N_HEADS)
    final_w = norm_final_w.reshape(1, d_model)

    hid = x.reshape(batch * seq, d_model)
    for layer in range(depth):
        w_in_l = w_in[layer].astype(BF16)
        nw = norm_mix_w[layer].reshape(1, d_model)
        qkv_a, qkvg_r = _inproj(hid, nw, w_in_l, 3, F32, BF16, batch, seq)
        attn, w_gate_up_l, w_down_l = _attention(
            slopes, qkv_a, w_gate[layer], w_up[layer], w_down[layer])
        ret, w_out_l = _retention(ret_decay_fwd[layer], ret_decay_bwd[layer], qkvg_r,
                                  ret_norm_w[layer].reshape(N_HEADS, 1, HEAD_DIM),
                                  w_out[layer])
        hid, n2 = _outproj(attn, ret, hid, w_out_l, norm_ffn_w[layer].reshape(1, d_model))
        last = layer == depth - 1
        hid = _ffn(n2, hid, w_gate_up_l, w_down_l, final_w, final_norm=last)
    return hid.reshape(batch, seq, d_model)
```

```python
import functools

import jax
import jax.numpy as jnp
from jax import lax
from jax.experimental import pallas as pl
from jax.experimental.pallas import tpu as pltpu

F32 = jnp.float32
BF16 = jnp.bfloat16

HEAD_DIM = 128
N_HEADS = 8
D_GROUP = N_HEADS * HEAD_DIM
DILATED_PATTERNS = ((128, 1), (512, 4), (2048, 16))
HALF_BAND = 64
RET_BLOCK = 256
EPS = 1e-6
NEG_BIG = -1e30
LOG2_E = 1.4426950408889634
MIB = 1024 * 1024

FFN_TF = 512
ATT_TQ = 128
ATT_TK = ATT_TQ + 2 * HALF_BAND


def _params(semantics, vmem_mib, flags=None):
    return pltpu.CompilerParams(dimension_semantics=semantics,
                                vmem_limit_bytes=vmem_mib * MIB, flags=flags)


def _to_bf16(x, mul=None):
    return (x if mul is None else x * mul).astype(BF16)


def _rmsnorm_rows(x, w):
    return x * lax.rsqrt(jnp.mean(x * x, axis=-1, keepdims=True) + EPS) * w


def _inproj_kernel(x_ref, nw_ref, w_ref, oa_ref, ob_ref, n_scr, *,
                   groups, groups_a, x_slices):
    i = pl.program_id(0)
    j = pl.program_id(1)
    slice_rows = x_ref.shape[0]

    def norm_slice():
        rows = pl.ds(pl.multiple_of(j * slice_rows, slice_rows), slice_rows)
        n_scr[i & 1, rows, :] = _rmsnorm_rows(x_ref[...], nw_ref[...]).astype(BF16)

    def project(o_ref, with_norm):
        res = jnp.dot(n_scr[(i - 1) & 1], w_ref[...], preferred_element_type=F32)
        for h in range(N_HEADS):
            o_ref[h] = res[:, h * HEAD_DIM:(h + 1) * HEAD_DIM].astype(o_ref.dtype)
        if with_norm:
            norm_slice()

    @pl.when(j == 0)
    def _():
        ob_ref[...] = jnp.zeros_like(ob_ref)

    @pl.when((i == 0) & (j == 0))
    def _():
        oa_ref[...] = jnp.zeros_like(oa_ref)

    pl.when((i == 0) & (j < x_slices))(norm_slice)
    cases = {}
    for step in range(groups):
        cases.setdefault((step < groups_a, step < x_slices), []).append(step)
    for (to_a, with_norm), steps in cases.items():
        pl.when((i > 0) & (j >= steps[0]) & (j <= steps[-1]))(
            functools.partial(project, oa_ref if to_a else ob_ref, with_norm))


def _inproj(x2, norm_w, w, groups_a, dtype_a, dtype_b, batch, seq, tm=1024, x_slices=4):
    n_tok, d_model = x2.shape
    groups = w.shape[1] // D_GROUP
    groups_b = groups - groups_a
    n_tiles = n_tok // tm
    s_blocks = seq // tm
    assert x_slices <= groups and tm % x_slices == 0

    def out_spec(group_of_j):
        def index(i, j):
            t = jnp.maximum(i - 1, 0)
            g = jnp.where(i == 0, 0, group_of_j(j))
            return (g, t // s_blocks, 0, t % s_blocks, 0)
        return pl.BlockSpec((None, None, N_HEADS, tm, HEAD_DIM), index)

    def out_shape(n_groups, dtype):
        return jax.ShapeDtypeStruct((n_groups, batch, N_HEADS, seq, HEAD_DIM), dtype)

    return pl.pallas_call(
        functools.partial(_inproj_kernel, groups=groups, groups_a=groups_a,
                          x_slices=x_slices),
        grid=(n_tiles + 1, groups),
        in_specs=[
            pl.BlockSpec((tm // x_slices, d_model),
                         lambda i, j: (jnp.minimum(i, n_tiles - 1) * x_slices
                                       + jnp.minimum(j, x_slices - 1), 0)),
            pl.BlockSpec((1, d_model), lambda i, j: (0, 0)),
            pl.BlockSpec((d_model, D_GROUP), lambda i, j: (0, jnp.where(i == 0, 0, j))),
        ],
        out_specs=[out_spec(lambda j: jnp.minimum(j, groups_a - 1)),
                   out_spec(lambda j: jnp.maximum(j - groups_a, 0))],
        out_shape=[out_shape(groups_a, dtype_a), out_shape(groups_b, dtype_b)],
        scratch_shapes=[pltpu.VMEM((2, tm, d_model), BF16)],
        compiler_params=_params(("arbitrary", "arbitrary"), 48),
        name="inproj",
    )(x2, norm_w, w)


def _attn_kernel(slopes_ref, q_ref, k_ref, v_ref, wg_ref, wu_ref,
                 o_ref, wg_bf, wu_bf,
                 q4f, k4f, v4f,
                 num4, den4, m4, num1, den1, m1, bias_s):
    for w_src, w_dst in ((wg_ref, wg_bf), (wu_ref, wu_bf)):
        w_dst[...] = w_src[...].astype(BF16)

    seq = q_ref.shape[0]
    quarter = seq // 4
    q_scale = HEAD_DIM ** -0.5 * LOG2_E

    @pl.when(pl.program_id(1) == 0)
    def _():
        slope = slopes_ref[pl.program_id(0)]
        row = lax.broadcasted_iota(jnp.int32, (ATT_TQ, ATT_TK), 0)
        col = lax.broadcasted_iota(jnp.int32, (ATT_TQ, ATT_TK), 1)
        for p, (_, dil) in enumerate(DILATED_PATTERNS):
            for c, off in enumerate((0, HALF_BAND, 2 * HALF_BAND)):
                dist = jnp.abs(row - col + off)
                bias = -slope * (dil * dist).astype(F32) * LOG2_E
                bias_s[p * 3 + c] = jnp.where(dist <= HALF_BAND, bias, NEG_BIG)

    for src, dst in ((q_ref, q4f), (k_ref, k4f), (v_ref, v4f)):
        for r in range(4):
            dst[r * quarter:(r + 1) * quarter, :] = src[pl.ds(r, quarter, stride=4), :]

    def run_pattern(p, dil, sources, rows_of, incoming, emit):
        q_src, k_src, v_src = sources
        sub_len = seq // dil
        n_blk = sub_len // ATT_TQ
        shift = n_blk.bit_length() - 1
        assert n_blk == 1 << shift and n_blk >= 2

        def block(idx, carry):
            r = idx >> shift
            blk = idx & (n_blk - 1)
            q0 = blk * ATT_TQ
            k0 = jnp.clip(q0 - HALF_BAND, 0, sub_len - ATT_TK)
            q_rows = pl.ds(pl.multiple_of(r * sub_len + q0, ATT_TQ), ATT_TQ)
            case = jnp.where(blk == 0, 0, jnp.where(blk == n_blk - 1, 2, 1))
            q = _to_bf16(q_src[rows_of(r, q0, ATT_TQ), :], q_scale)
            k = _to_bf16(k_src[rows_of(r, k0, ATT_TK), :])
            v = _to_bf16(v_src[rows_of(r, k0, ATT_TK), :])
            s = lax.dot_general(q, k, (((1,), (1,)), ((), ())),
                                preferred_element_type=F32)
            s = s + bias_s[p * 3 + case]
            m_blk = jnp.max(s, axis=-1, keepdims=True)
            if incoming is None:
                m_new = jnp.broadcast_to(m_blk, (ATT_TQ, HEAD_DIM))
            else:
                num_in, den_in, m_in = incoming
                m_old = m_in[q_rows, :]
                m_new = jnp.maximum(m_old, m_blk)
            e = jnp.exp2(s - jnp.concatenate([m_new, m_new], axis=-1))
            pv = jnp.dot(e.astype(BF16), jnp.concatenate([v, jnp.ones_like(v)], axis=-1),
                         preferred_element_type=F32)
            num = pv[:, :HEAD_DIM]
            den = pv[:, HEAD_DIM:]
            if incoming is not None:
                alpha = jnp.exp2(m_old - m_new)
                num = alpha * num_in[q_rows, :] + num
                den = alpha * den_in[q_rows, :] + den
            emit(r, q0, num, den, m_new)
            return carry

        lax.fori_loop(0, dil * n_blk, block, 0, unroll=32)

    copies4 = (q4f, k4f, v4f)

    def rows16(r, l0, n):
        return pl.ds((r & 3) * quarter + 4 * l0 + (r >> 2), n, stride=4)

    def emit16(r, q0, num, den, m_new):
        rows = pl.ds((r & 3) * quarter + 4 * q0 + (r >> 2), ATT_TQ, stride=4)
        num4[rows, :] = num
        den4[rows, :] = den
        m4[rows, :] = m_new

    run_pattern(2, 16, copies4, rows16, None, emit16)

    def rows4(r, l0, n):
        return pl.ds(pl.multiple_of(r * quarter + l0, HALF_BAND), n)

    def emit4(r, q0, num, den, m_new):
        rows = pl.ds(4 * q0 + r, ATT_TQ, stride=4)
        num1[rows, :] = num
        den1[rows, :] = den
        m1[rows, :] = m_new

    run_pattern(1, 4, copies4, rows4, (num4, den4, m4), emit4)

    def rows1(r, l0, n):
        return pl.ds(pl.multiple_of(l0, HALF_BAND), n)

    def emit1(r, q0, num, den, m_new):
        o_ref[pl.ds(pl.multiple_of(q0, ATT_TQ), ATT_TQ), :] = (num / den).astype(o_ref.dtype)

    run_pattern(0, 1, (q_ref, k_ref, v_ref), rows1, (num1, den1, m1), emit1)


def _cast_specs(weights, n_steps, step_of):
    specs, shapes = [], []
    for w in weights:
        rows = w.shape[0] // n_steps
        assert rows * n_steps == w.shape[0] and rows % 16 == 0
        specs.append(pl.BlockSpec((rows, w.shape[1]), lambda *idx: (step_of(*idx), 0)))
        shapes.append(jax.ShapeDtypeStruct(w.shape, BF16))
    return specs, shapes


def _attention(slopes, qkv, ffn_weights):
    _, batch, heads, seq, dh = qkv.shape
    assert seq % (16 * ATT_TK) == 0 and dh == HEAD_DIM

    def spec(t):
        return pl.BlockSpec((None, None, None, seq, dh),
                            lambda h, b, t=t: (t, b, h, 0, 0))

    w_specs, w_shapes = _cast_specs(ffn_weights, heads * batch, lambda h, b: h * batch + b)
    f32_rows = pltpu.VMEM((seq, dh), F32)
    return pl.pallas_call(
        _attn_kernel,
        grid=(heads, batch),
        in_specs=[pl.BlockSpec(memory_space=pltpu.SMEM), spec(0), spec(1), spec(2)] + w_specs,
        out_specs=[pl.BlockSpec((None, None, seq, dh), lambda h, b: (b, h, 0, 0))] + w_specs,
        out_shape=[jax.ShapeDtypeStruct((batch, heads, seq, dh), BF16)] + w_shapes,
        scratch_shapes=[
            f32_rows, f32_rows, f32_rows,
            f32_rows, f32_rows, f32_rows, f32_rows, f32_rows, f32_rows,
            pltpu.VMEM((3 * len(DILATED_PATTERNS), ATT_TQ, ATT_TK), F32),
        ],
        compiler_params=_params(("parallel", "arbitrary"), 56),
        name="dilated_attention",
    )(slopes, qkv, qkv, qkv, *ffn_weights)


def _ret_kernel(df_ref, db_ref, q_ref, k_ref, v_ref, g_ref, rw_ref, wo_ref, wd_ref,
                o_ref, wo_bf, wd_bf, intra_s, tab, kv_s, st_s):
    wo_bf[...] = wo_ref[...].astype(BF16)
    wd_bf[...] = wd_ref[...].astype(BF16)

    seq = q_ref.shape[0]
    ch = RET_BLOCK
    n_ch = seq // ch
    h = pl.program_id(0)
    scale = HEAD_DIM ** -0.5
    T_QF, T_QB, T_KF, T_KB = range(4)

    lg_f = -jnp.exp(jnp.full((1, HEAD_DIM), df_ref[h], F32))
    lg_b = -jnp.exp(jnp.full((1, HEAD_DIM), db_ref[h], F32))

    @pl.when(pl.program_id(1) == 0)
    def _():
        rel = (lax.broadcasted_iota(jnp.int32, (ch, ch), 0)
               - lax.broadcasted_iota(jnp.int32, (ch, ch), 1)).astype(F32)
        intra_s[...] = jnp.where(rel >= 0.0,
                                 jnp.exp(lg_f[:, :1] * jnp.maximum(rel, 0.0)),
                                 jnp.exp(lg_b[:, :1] * jnp.maximum(-rel, 0.0))) * scale
        ri = lax.broadcasted_iota(jnp.int32, (ch, HEAD_DIM), 0).astype(F32)
        tab[T_QF] = jnp.exp(lg_f * (ri + 1.0)) * scale
        tab[T_QB] = jnp.exp(lg_b * (ch - ri)) * scale
        tab[T_KF] = jnp.exp(lg_f * (ch - 1.0 - ri))
        tab[T_KB] = jnp.exp(lg_b * ri)

    cdec_f = jnp.exp(lg_f * ch)
    cdec_b = jnp.exp(lg_b * ch)

    def chunk_rows(c):
        return pl.ds(pl.multiple_of(c * ch, ch), ch)

    def kv_step(c, carry):
        rows = chunk_rows(c)
        vc = v_ref[rows, :].astype(F32)
        v2 = jnp.concatenate([vc * tab[T_KF], vc * tab[T_KB]], axis=-1).astype(BF16)
        kv_s[c] = lax.dot_general(k_ref[rows, :], v2, (((0,), (0,)), ((), ())),
                                  preferred_element_type=F32)
        return carry

    lax.fori_loop(0, n_ch, kv_step, 0, unroll=n_ch)
    zero_state = jnp.zeros((HEAD_DIM, HEAD_DIM), F32)

    def scan_f(c, state):
        st_s[c, :, :HEAD_DIM] = state.astype(BF16)
        return state * cdec_f + kv_s[c, :, :HEAD_DIM]

    def scan_b(i, state):
        c = n_ch - 1 - i
        st_s[c, :, HEAD_DIM:] = state.astype(BF16)
        return state * cdec_b + kv_s[c, :, HEAD_DIM:]

    lax.fori_loop(0, n_ch, scan_f, zero_state, unroll=n_ch)
    lax.fori_loop(0, n_ch, scan_b, zero_state, unroll=n_ch)

    def out_step(c, carry):
        rows = chunk_rows(c)
        qb = q_ref[rows, :]
        inner = lax.dot_general(qb, k_ref[rows, :], (((1,), (1,)), ((), ())),
                                preferred_element_type=F32) * intra_s[...]
        o = jnp.dot(inner.astype(BF16), v_ref[rows, :], preferred_element_type=F32)
        cross = jnp.dot(qb, st_s[c], preferred_element_type=F32)
        o = o + cross[:, :HEAD_DIM] * tab[T_QF] + cross[:, HEAD_DIM:] * tab[T_QB]
        o = o * lax.rsqrt(jnp.mean(o * o, axis=-1, keepdims=True) + EPS)
        o = o * rw_ref[...]
        g = g_ref[rows, :].astype(F32)
        o = o * (g * (1.0 / (1.0 + jnp.exp(-g))))
        o_ref[rows, :] = o.astype(o_ref.dtype)
        return carry

    lax.fori_loop(0, n_ch, out_step, 0, unroll=n_ch)


def _retention(decay_f, decay_b, qkvg, ret_norm_w, w_out, w_down):
    _, batch, heads, seq, dh = qkvg.shape

    def spec(t):
        return pl.BlockSpec((None, None, None, seq, dh),
                            lambda h, b, t=t: (t, b, h, 0, 0))

    smem = pl.BlockSpec(memory_space=pltpu.SMEM)
    w_specs, w_shapes = _cast_specs([w_out, w_down], heads * batch,
                                    lambda h, b: h * batch + b)
    return pl.pallas_call(
        _ret_kernel,
        grid=(heads, batch),
        in_specs=[smem, smem, spec(0), spec(1), spec(2), spec(3),
                  pl.BlockSpec((None, 1, dh), lambda h, b: (h, 0, 0))] + w_specs,
        out_specs=[pl.BlockSpec((None, None, seq, dh), lambda h, b: (b, h, 0, 0))] + w_specs,
        out_shape=[jax.ShapeDtypeStruct((batch, heads, seq, dh), BF16)] + w_shapes,
        scratch_shapes=[pltpu.VMEM((RET_BLOCK, RET_BLOCK), F32),
                        pltpu.VMEM((4, RET_BLOCK, dh), F32),
                        pltpu.VMEM((seq // RET_BLOCK, dh, 2 * dh), F32),
                        pltpu.VMEM((seq // RET_BLOCK, dh, 2 * dh), BF16)],
        compiler_params=_params(("parallel", "arbitrary"), 32),
        name="retention",
    )(decay_f, decay_b, qkvg, qkvg, qkvg, qkvg, ret_norm_w, w_out, w_down)


def _outproj_kernel(a_ref, r_ref, x_ref, w_ref, nw_ref, h_ref, n_ref):
    mixed = jnp.concatenate([a_ref[h] for h in range(N_HEADS)]
                            + [r_ref[h] for h in range(N_HEADS)], axis=-1)
    hid = x_ref[...] + jnp.dot(mixed, w_ref[...], preferred_element_type=F32)
    h_ref[...] = hid
    n_ref[...] = _rmsnorm_rows(hid, nw_ref[...]).astype(n_ref.dtype)


def _outproj(attn, ret, x2, w_out, norm_w, tm=512):
    batch, heads, seq, dh = attn.shape
    n_tok, d_model = x2.shape
    s_blocks = seq // tm
    head_spec = pl.BlockSpec((None, heads, tm, dh),
                             lambda i: (i // s_blocks, 0, i % s_blocks, 0))
    row_spec = pl.BlockSpec((tm, d_model), lambda i: (i, 0))
    return pl.pallas_call(
        _outproj_kernel,
        grid=(n_tok // tm,),
        in_specs=[head_spec, head_spec, row_spec,
                  pl.BlockSpec(w_out.shape, lambda i: (0, 0)),
                  pl.BlockSpec((1, d_model), lambda i: (0, 0))],
        out_specs=[row_spec, row_spec],
        out_shape=[jax.ShapeDtypeStruct((n_tok, d_model), F32),
                   jax.ShapeDtypeStruct((n_tok, d_model), BF16)],
        compiler_params=_params(("parallel",), 56),
        name="outproj",
    )(attn, ret, x2, w_out, norm_w)


def _ffn_kernel(n_ref, h_ref, wg_ref, wu_ref, wd_ref, fw_ref, o_ref, *,
                final_norm, h_slices):
    f = pl.program_id(1)
    h_rows = h_ref.shape[0]

    @pl.when(f == 0)
    def _():
        o_ref[...] = jnp.zeros_like(o_ref)

    n = n_ref[...]
    g = jnp.dot(n, wg_ref[...], preferred_element_type=F32)
    u = jnp.dot(n, wu_ref[...], preferred_element_type=F32)
    a = (g * (1.0 / (1.0 + jnp.exp(-g))) * u).astype(BF16)
    o_ref[...] += jnp.dot(a, wd_ref[...], preferred_element_type=F32)

    slice_idx = jnp.minimum(f, h_slices - 1)
    rows = pl.ds(pl.multiple_of(slice_idx * h_rows, h_rows), h_rows)
    o_ref[rows, :] += h_ref[...] * (f < h_slices).astype(F32)

    if final_norm:
        @pl.when(f == pl.num_programs(1) - 1)
        def _():
            o_ref[...] = _rmsnorm_rows(o_ref[...], fw_ref[...])


def _ffn(n2, hid, w_gate, w_up, w_down, final_w, final_norm, tm=1024, tf=FFN_TF,
         h_slices=8):
    n_tok, d_model = hid.shape
    f_steps = w_gate.shape[1] // tf
    assert f_steps >= h_slices and tm % h_slices == 0
    row = lambda i, f: (i, 0)
    col_tile = pl.BlockSpec((d_model, tf), lambda i, f: (0, f))
    return pl.pallas_call(
        functools.partial(_ffn_kernel, final_norm=final_norm, h_slices=h_slices),
        grid=(n_tok // tm, f_steps),
        in_specs=[pl.BlockSpec((tm, d_model), row),
                  pl.BlockSpec((tm // h_slices, d_model),
                               lambda i, f: (i * h_slices + jnp.minimum(f, h_slices - 1), 0)),
                  col_tile, col_tile,
                  pl.BlockSpec((tf, d_model), lambda i, f: (f, 0)),
                  pl.BlockSpec((1, d_model), lambda i, f: (0, 0))],
        out_specs=pl.BlockSpec((tm, d_model), row),
        out_shape=jax.ShapeDtypeStruct((n_tok, d_model), F32),
        compiler_params=_params(("parallel", "arbitrary"), 56),
        name="ffn",
    )(n2, hid, w_gate, w_up, w_down, final_w)


def kernel(x, norm_mix_w, w_in, ret_decay_fwd, ret_decay_bwd, ret_norm_w, w_out,
           norm_ffn_w, w_gate, w_up, w_down, norm_final_w):
    batch, seq, d_model = x.shape
    depth = w_in.shape[0]
    assert depth >= 1
    slopes = jnp.exp2(-8.0 * jnp.arange(1, N_HEADS + 1, dtype=F32) / N_HEADS)
    final_w = norm_final_w.reshape(1, d_model)

    hid = x.reshape(batch * seq, d_model)
    for layer in range(depth):
        w_in_l = w_in[layer].astype(BF16)
        nw = norm_mix_w[layer].reshape(1, d_model)
        qkv_a, qkvg_r = _inproj(hid, nw, w_in_l, 3, F32, BF16, batch, seq)
        attn, w_gate_l, w_up_l = _attention(slopes, qkv_a, [w_gate[layer], w_up[layer]])
        ret, w_out_l, w_down_l = _retention(
            ret_decay_fwd[layer], ret_decay_bwd[layer], qkvg_r,
            ret_norm_w[layer].reshape(N_HEADS, 1, HEAD_DIM), w_out[layer], w_down[layer])
        hid, n2 = _outproj(attn, ret, hid, w_out_l, norm_ffn_w[layer].reshape(1, d_model))
        last = layer == depth - 1
        hid = _ffn(n2, hid, w_gate_l, w_up_l, w_down_l, final_w, final_norm=last)
    return hid.reshape(batch, seq, d_model)
```

```python
import functools

import jax
import jax.numpy as jnp
from jax import lax
from jax.experimental import pallas as pl
from jax.experimental.pallas import tpu as pltpu

F32 = jnp.float32
BF16 = jnp.bfloat16

HEAD_DIM = 128
N_HEADS = 8
D_GROUP = N_HEADS * HEAD_DIM
DILATED_PATTERNS = ((128, 1), (512, 4), (2048, 16))
HALF_BAND = 64
RET_BLOCK = 256
EPS = 1e-6
NEG_BIG = -1e30
LOG2_E = 1.4426950408889634
MIB = 1024 * 1024

FFN_TF = 512
ATT_TQ = 128
ATT_TK = ATT_TQ + 2 * HALF_BAND


def _params(semantics, vmem_mib, flags=None):
    return pltpu.CompilerParams(dimension_semantics=semantics,
                                vmem_limit_bytes=vmem_mib * MIB, flags=flags)


def _to_bf16(x, mul=None):
    return (x if mul is None else x * mul).astype(BF16)


def _rmsnorm_rows(x, w):
    return x * lax.rsqrt(jnp.mean(x * x, axis=-1, keepdims=True) + EPS) * w


def _inproj_kernel(x_ref, nw_ref, w_ref, oa_ref, ob_ref, n_scr, *,
                   groups, groups_a, x_slices):
    i = pl.program_id(0)
    j = pl.program_id(1)
    slice_rows = x_ref.shape[0]

    def norm_slice():
        rows = pl.ds(pl.multiple_of(j * slice_rows, slice_rows), slice_rows)
        n_scr[i & 1, rows, :] = _rmsnorm_rows(x_ref[...], nw_ref[...]).astype(BF16)

    def project(o_ref, with_norm):
        res = jnp.dot(n_scr[(i - 1) & 1], w_ref[...], preferred_element_type=F32)
        for h in range(N_HEADS):
            o_ref[h] = res[:, h * HEAD_DIM:(h + 1) * HEAD_DIM].astype(o_ref.dtype)
        if with_norm:
            norm_slice()

    @pl.when(j == 0)
    def _():
        ob_ref[...] = jnp.zeros_like(ob_ref)

    @pl.when((i == 0) & (j == 0))
    def _():
        oa_ref[...] = jnp.zeros_like(oa_ref)

    pl.when((i == 0) & (j < x_slices))(norm_slice)
    cases = {}
    for step in range(groups):
        cases.setdefault((step < groups_a, step < x_slices), []).append(step)
    for (to_a, with_norm), steps in cases.items():
        pl.when((i > 0) & (j >= steps[0]) & (j <= steps[-1]))(
            functools.partial(project, oa_ref if to_a else ob_ref, with_norm))


def _inproj(x2, norm_w, w, groups_a, dtype_a, dtype_b, batch, seq, tm=1024, x_slices=4):
    n_tok, d_model = x2.shape
    groups = w.shape[1] // D_GROUP
    groups_b = groups - groups_a
    n_tiles = n_tok // tm
    s_blocks = seq // tm
    assert x_slices <= groups and tm % x_slices == 0

    def out_spec(group_of_j):
        def index(i, j):
            t = jnp.maximum(i - 1, 0)
            g = jnp.where(i == 0, 0, group_of_j(j))
            return (g, t // s_blocks, 0, t % s_blocks, 0)
        return pl.BlockSpec((None, None, N_HEADS, tm, HEAD_DIM), index)

    def out_shape(n_groups, dtype):
        return jax.ShapeDtypeStruct((n_groups, batch, N_HEADS, seq, HEAD_DIM), dtype)

    return pl.pallas_call(
        functools.partial(_inproj_kernel, groups=groups, groups_a=groups_a,
                          x_slices=x_slices),
        grid=(n_tiles + 1, groups),
        in_specs=[
            pl.BlockSpec((tm // x_slices, d_model),
                         lambda i, j: (jnp.minimum(i, n_tiles - 1) * x_slices
                                       + jnp.minimum(j, x_slices - 1), 0)),
            pl.BlockSpec((1, d_model), lambda i, j: (0, 0)),
            pl.BlockSpec((d_model, D_GROUP), lambda i, j: (0, jnp.where(i == 0, 0, j))),
        ],
        out_specs=[out_spec(lambda j: jnp.minimum(j, groups_a - 1)),
                   out_spec(lambda j: jnp.maximum(j - groups_a, 0))],
        out_shape=[out_shape(groups_a, dtype_a), out_shape(groups_b, dtype_b)],
        scratch_shapes=[pltpu.VMEM((2, tm, d_model), BF16)],
        compiler_params=_params(("arbitrary", "arbitrary"), 48),
        name="inproj",
    )(x2, norm_w, w)


def _attn_kernel(slopes_ref, q_ref, k_ref, v_ref, wg_ref, wu_ref,
                 o_ref, wg_bf, wu_bf,
                 q4f, k4f, v4f,
                 num4, den4, m4, num1, den1, m1, bias_s):
    for w_src, w_dst in ((wg_ref, wg_bf), (wu_ref, wu_bf)):
        w_dst[...] = w_src[...].astype(BF16)

    seq = q_ref.shape[0]
    quarter = seq // 4
    q_scale = HEAD_DIM ** -0.5 * LOG2_E

    @pl.when(pl.program_id(1) == 0)
    def _():
        slope = slopes_ref[pl.program_id(0)]
        row = lax.broadcasted_iota(jnp.int32, (ATT_TQ, ATT_TK), 0)
        col = lax.broadcasted_iota(jnp.int32, (ATT_TQ, ATT_TK), 1)
        for p, (_, dil) in enumerate(DILATED_PATTERNS):
            for c, off in enumerate((0, HALF_BAND, 2 * HALF_BAND)):
                dist = jnp.abs(row - col + off)
                bias = -slope * (dil * dist).astype(F32) * LOG2_E
                bias_s[p * 3 + c] = jnp.where(dist <= HALF_BAND, bias, NEG_BIG)

    for src, dst in ((q_ref, q4f), (k_ref, k4f), (v_ref, v4f)):
        for r in range(4):
            dst[r * quarter:(r + 1) * quarter, :] = src[pl.ds(r, quarter, stride=4), :]

    def run_pattern(p, dil, sources, rows_of, incoming, emit):
        q_src, k_src, v_src = sources
        sub_len = seq // dil
        n_blk = sub_len // ATT_TQ
        shift = n_blk.bit_length() - 1
        assert n_blk == 1 << shift and n_blk >= 2

        def block(idx, carry):
            r = idx >> shift
            blk = idx & (n_blk - 1)
            q0 = blk * ATT_TQ
            k0 = jnp.clip(q0 - HALF_BAND, 0, sub_len - ATT_TK)
            q_rows = pl.ds(pl.multiple_of(r * sub_len + q0, ATT_TQ), ATT_TQ)
            case = jnp.where(blk == 0, 0, jnp.where(blk == n_blk - 1, 2, 1))
            q = _to_bf16(q_src[rows_of(r, q0, ATT_TQ), :], q_scale)
            k = _to_bf16(k_src[rows_of(r, k0, ATT_TK), :])
            v = _to_bf16(v_src[rows_of(r, k0, ATT_TK), :])
            s = lax.dot_general(q, k, (((1,), (1,)), ((), ())),
                                preferred_element_type=F32)
            s = s + bias_s[p * 3 + case]
            m_blk = jnp.max(s, axis=-1, keepdims=True)
            if incoming is None:
                m_new = jnp.broadcast_to(m_blk, (ATT_TQ, HEAD_DIM))
            else:
                num_in, den_in, m_in = incoming
                m_old = m_in[q_rows, :]
                m_new = jnp.maximum(m_old, m_blk)
            e = jnp.exp2(s - jnp.concatenate([m_new, m_new], axis=-1))
            pv = jnp.dot(e.astype(BF16), jnp.concatenate([v, jnp.ones_like(v)], axis=-1),
                         preferred_element_type=F32)
            num = pv[:, :HEAD_DIM]
            den = pv[:, HEAD_DIM:]
            if incoming is not None:
                alpha = jnp.exp2(m_old - m_new)
                num = alpha * num_in[q_rows, :] + num
                den = alpha * den_in[q_rows, :] + den
            emit(r, q0, num, den, m_new)
            return carry

        lax.fori_loop(0, dil * n_blk, block, 0, unroll=32)

    copies4 = (q4f, k4f, v4f)

    def rows16(r, l0, n):
        return pl.ds((r & 3) * quarter + 4 * l0 + (r >> 2), n, stride=4)

    def emit16(r, q0, num, den, m_new):
        rows = pl.ds((r & 3) * quarter + 4 * q0 + (r >> 2), ATT_TQ, stride=4)
        num4[rows, :] = num
        den4[rows, :] = den
        m4[rows, :] = m_new

    run_pattern(2, 16, copies4, rows16, None, emit16)

    def rows4(r, l0, n):
        return pl.ds(pl.multiple_of(r * quarter + l0, HALF_BAND), n)

    def emit4(r, q0, num, den, m_new):
        rows = pl.ds(4 * q0 + r, ATT_TQ, stride=4)
        num1[rows, :] = num
        den1[rows, :] = den
        m1[rows, :] = m_new

    run_pattern(1, 4, copies4, rows4, (num4, den4, m4), emit4)

    def rows1(r, l0, n):
        return pl.ds(pl.multiple_of(l0, HALF_BAND), n)

    def emit1(r, q0, num, den, m_new):
        o_ref[pl.ds(pl.multiple_of(q0, ATT_TQ), ATT_TQ), :] = (num / den).astype(o_ref.dtype)

    run_pattern(0, 1, (q_ref, k_ref, v_ref), rows1, (num1, den1, m1), emit1)


def _cast_specs(weights, n_steps, step_of):
    specs, shapes = [], []
    for w in weights:
        rows = w.shape[0] // n_steps
        assert rows * n_steps == w.shape[0] and rows % 16 == 0
        specs.append(pl.BlockSpec((rows, w.shape[1]), lambda *idx: (step_of(*idx), 0)))
        shapes.append(jax.ShapeDtypeStruct(w.shape, BF16))
    return specs, shapes


def _attention(slopes, qkv, ffn_weights):
    _, batch, heads, seq, dh = qkv.shape
    assert seq % (16 * ATT_TK) == 0 and dh == HEAD_DIM

    def spec(t):
        return pl.BlockSpec((None, None, None, seq, dh),
                            lambda h, b, t=t: (t, b, h, 0, 0))

    w_specs, w_shapes = _cast_specs(ffn_weights, heads * batch, lambda h, b: h * batch + b)
    f32_rows = pltpu.VMEM((seq, dh), F32)
    return pl.pallas_call(
        _attn_kernel,
        grid=(heads, batch),
        in_specs=[pl.BlockSpec(memory_space=pltpu.SMEM), spec(0), spec(1), spec(2)] + w_specs,
        out_specs=[pl.BlockSpec((None, None, seq, dh), lambda h, b: (b, h, 0, 0))] + w_specs,
        out_shape=[jax.ShapeDtypeStruct((batch, heads, seq, dh), BF16)] + w_shapes,
        scratch_shapes=[
            f32_rows, f32_rows, f32_rows,
            f32_rows, f32_rows, f32_rows, f32_rows, f32_rows, f32_rows,
            pltpu.VMEM((3 * len(DILATED_PATTERNS), ATT_TQ, ATT_TK), F32),
        ],
        compiler_params=_params(("parallel", "arbitrary"), 56),
        name="dilated_attention",
    )(slopes, qkv, qkv, qkv, *ffn_weights)


def _ret_kernel(df_ref, db_ref, q_ref, k_ref, v_ref, g_ref, rw_ref, wo_ref, wd_ref,
                o_ref, wo_bf, wd_bf, intra_s, tab, kv_s, st_s):
    wo_bf[...] = wo_ref[...].astype(BF16)
    wd_bf[...] = wd_ref[...].astype(BF16)

    seq = q_ref.shape[0]
    ch = RET_BLOCK
    n_ch = seq // ch
    h = pl.program_id(0)
    scale = HEAD_DIM ** -0.5
    T_QF, T_QB, T_KF, T_KB = range(4)

    lg_f = -jnp.exp(jnp.full((1, HEAD_DIM), df_ref[h], F32))
    lg_b = -jnp.exp(jnp.full((1, HEAD_DIM), db_ref[h], F32))

    @pl.when(pl.program_id(1) == 0)
    def _():
        rel = (lax.broadcasted_iota(jnp.int32, (ch, ch), 0)
               - lax.broadcasted_iota(jnp.int32, (ch, ch), 1)).astype(F32)
        intra_s[...] = jnp.where(rel >= 0.0,
                                 jnp.exp(lg_f[:, :1] * jnp.maximum(rel, 0.0)),
                                 jnp.exp(lg_b[:, :1] * jnp.maximum(-rel, 0.0))) * scale
        ri = lax.broadcasted_iota(jnp.int32, (ch, HEAD_DIM), 0).astype(F32)
        tab[T_QF] = jnp.exp(lg_f * (ri + 1.0)) * scale
        tab[T_QB] = jnp.exp(lg_b * (ch - ri)) * scale
        tab[T_KF] = jnp.exp(lg_f * (ch - 1.0 - ri))
        tab[T_KB] = jnp.exp(lg_b * ri)

    cdec_f = jnp.exp(lg_f * ch)
    cdec_b = jnp.exp(lg_b * ch)

    def chunk_rows(c):
        return pl.ds(pl.multiple_of(c * ch, ch), ch)

    def kv_step(c, carry):
        rows = chunk_rows(c)
        vc = v_ref[rows, :].astype(F32)
        v2 = jnp.concatenate([vc * tab[T_KF], vc * tab[T_KB]], axis=-1).astype(BF16)
        kv_s[c] = lax.dot_general(k_ref[rows, :], v2, (((0,), (0,)), ((), ())),
                                  preferred_element_type=F32)
        return carry

    lax.fori_loop(0, n_ch, kv_step, 0, unroll=n_ch)
    zero_state = jnp.zeros((HEAD_DIM, HEAD_DIM), F32)

    def scan_f(c, state):
        st_s[c, :, :HEAD_DIM] = state.astype(BF16)
        return state * cdec_f + kv_s[c, :, :HEAD_DIM]

    def scan_b(i, state):
        c = n_ch - 1 - i
        st_s[c, :, HEAD_DIM:] = state.astype(BF16)
        return state * cdec_b + kv_s[c, :, HEAD_DIM:]

    lax.fori_loop(0, n_ch, scan_f, zero_state, unroll=n_ch)
    lax.fori_loop(0, n_ch, scan_b, zero_state, unroll=n_ch)

    def out_step(c, carry):
        rows = chunk_rows(c)
        qb = q_ref[rows, :]
        inner = lax.dot_general(qb, k_ref[rows, :], (((1,), (1,)), ((), ())),
                                preferred_element_type=F32) * intra_s[...]
        o = jnp.dot(inner.astype(BF16), v_ref[rows, :], preferred_element_type=F32)
        cross = jnp.dot(qb, st_s[c], preferred_element_type=F32)
        o = o + cross[:, :HEAD_DIM] * tab[T_QF] + cross[:, HEAD_DIM:] * tab[T_QB]
        o = o * lax.rsqrt(jnp.mean(o * o, axis=-1, keepdims=True) + EPS)
        o = o * rw_ref[...]
        g = g_ref[rows, :].astype(F32)
        o = o * (g * (1.0 / (1.0 + jnp.exp(-g))))
        o_ref[rows, :] = o.astype(o_ref.dtype)
        return carry

    lax.fori_loop(0, n_ch, out_step, 0, unroll=n_ch)


def _retention(decay_f, decay_b, qkvg, ret_norm_w, w_out, w_down):
    _, batch, heads, seq, dh = qkvg.shape

    def spec(t):
        return pl.BlockSpec((None, None, None, seq, dh),
                            lambda h, b, t=t: (t, b, h, 0, 0))

    smem = pl.BlockSpec(memory_space=pltpu.SMEM)
    w_specs, w_shapes = _cast_specs([w_out, w_down], heads * batch,
                                    lambda h, b: h * batch + b)
    return pl.pallas_call(
        _ret_kernel,
        grid=(heads, batch),
        in_specs=[smem, smem, spec(0), spec(1), spec(2), spec(3),
                  pl.BlockSpec((None, 1, dh), lambda h, b: (h, 0, 0))] + w_specs,
        out_specs=[pl.BlockSpec((None, None, seq, dh), lambda h, b: (b, h, 0, 0))] + w_specs,
        out_shape=[jax.ShapeDtypeStruct((batch, heads, seq, dh), BF16)] + w_shapes,
        scratch_shapes=[pltpu.VMEM((RET_BLOCK, RET_BLOCK), F32),
                        pltpu.VMEM((4, RET_BLOCK, dh), F32),
                        pltpu.VMEM((seq // RET_BLOCK, dh, 2 * dh), F32),
                        pltpu.VMEM((seq // RET_BLOCK, dh, 2 * dh), BF16)],
        compiler_params=_params(("parallel", "arbitrary"), 32),
        name="retention",
    )(decay_f, decay_b, qkvg, qkvg, qkvg, qkvg, ret_norm_w, w_out, w_down)


def _outproj_kernel(a_ref, r_ref, x_ref, w_ref, nw_ref, h_ref, n_ref):
    mixed = jnp.concatenate([a_ref[h] for h in range(N_HEADS)]
                            + [r_ref[h] for h in range(N_HEADS)], axis=-1)
    hid = x_ref[...] + jnp.dot(mixed, w_ref[...], preferred_element_type=F32)
    h_ref[...] = hid
    n_ref[...] = _rmsnorm_rows(hid, nw_ref[...]).astype(n_ref.dtype)


def _outproj(attn, ret, x2, w_out, norm_w, tm=512):
    batch, heads, seq, dh = attn.shape
    n_tok, d_model = x2.shape
    s_blocks = seq // tm
    head_spec = pl.BlockSpec((None, heads, tm, dh),
                             lambda i: (i // s_blocks, 0, i % s_blocks, 0))
    row_spec = pl.BlockSpec((tm, d_model), lambda i: (i, 0))
    return pl.pallas_call(
        _outproj_kernel,
        grid=(n_tok // tm,),
        in_specs=[head_spec, head_spec, row_spec,
                  pl.BlockSpec(w_out.shape, lambda i: (0, 0)),
                  pl.BlockSpec((1, d_model), lambda i: (0, 0))],
        out_specs=[row_spec, row_spec],
        out_shape=[jax.ShapeDtypeStruct((n_tok, d_model), F32),
                   jax.ShapeDtypeStruct((n_tok, d_model), BF16)],
        compiler_params=_params(("parallel",), 56),
        name="outproj",
    )(attn, ret, x2, w_out, norm_w)


def _ffn_kernel(n_ref, h_ref, wg_ref, wu_ref, wd_ref, fw_ref, o_ref, *,
                final_norm, h_slices):
    f = pl.program_id(1)
    h_rows = h_ref.shape[0]

    def add_residual():
        slice_idx = jnp.minimum(f, h_slices - 1)
        rows = pl.ds(pl.multiple_of(slice_idx * h_rows, h_rows), h_rows)
        o_ref[rows, :] += h_ref[...] * (f < h_slices).astype(F32)

    def ffn_tile():
        n = n_ref[...]
        g = jnp.dot(n, wg_ref[...], preferred_element_type=F32)
        u = jnp.dot(n, wu_ref[...], preferred_element_type=F32)
        a = (g * (1.0 / (1.0 + jnp.exp(-g))) * u).astype(BF16)
        return jnp.dot(a, wd_ref[...], preferred_element_type=F32)

    @pl.when(f == 0)
    def _():
        o_ref[...] = ffn_tile()
        add_residual()

    @pl.when(f > 0)
    def _():
        add_residual()
        o_ref[...] += ffn_tile()

    if final_norm:
        @pl.when(f == pl.num_programs(1) - 1)
        def _():
            o_ref[...] = _rmsnorm_rows(o_ref[...], fw_ref[...])


def _ffn(n2, hid, w_gate, w_up, w_down, final_w, final_norm, tm=1024, tf=FFN_TF,
         h_slices=8):
    n_tok, d_model = hid.shape
    f_steps = w_gate.shape[1] // tf
    assert f_steps >= h_slices and tm % h_slices == 0
    row = lambda i, f: (i, 0)
    col_tile = pl.BlockSpec((d_model, tf), lambda i, f: (0, f))
    return pl.pallas_call(
        functools.partial(_ffn_kernel, final_norm=final_norm, h_slices=h_slices),
        grid=(n_tok // tm, f_steps),
        in_specs=[pl.BlockSpec((tm, d_model), row),
                  pl.BlockSpec((tm // h_slices, d_model),
                               lambda i, f: (i * h_slices + jnp.minimum(f, h_slices - 1), 0)),
                  col_tile, col_tile,
                  pl.BlockSpec((tf, d_model), lambda i, f: (f, 0)),
                  pl.BlockSpec((1, d_model), lambda i, f: (0, 0))],
        out_specs=pl.BlockSpec((tm, d_model), row),
        out_shape=jax.ShapeDtypeStruct((n_tok, d_model), F32),
        compiler_params=_params(("parallel", "arbitrary"), 56),
        name="ffn",
    )(n2, hid, w_gate, w_up, w_down, final_w)


def kernel(x, norm_mix_w, w_in, ret_decay_fwd, ret_decay_bwd, ret_norm_w, w_out,
           norm_ffn_w, w_gate, w_up, w_down, norm_final_w):
    batch, seq, d_model = x.shape
    depth = w_in.shape[0]
    assert depth >= 1
    slopes = jnp.exp2(-8.0 * jnp.arange(1, N_HEADS + 1, dtype=F32) / N_HEADS)
    final_w = norm_final_w.reshape(1, d_model)

    hid = x.reshape(batch * seq, d_model)
    for layer in range(depth):
        w_in_l = w_in[layer].astype(BF16)
        nw = norm_mix_w[layer].reshape(1, d_model)
        qkv_a, qkvg_r = _inproj(hid, nw, w_in_l, 3, F32, BF16, batch, seq)
        attn, w_gate_l, w_up_l = _attention(slopes, qkv_a, [w_gate[layer], w_up[layer]])
        ret, w_out_l, w_down_l = _retention(
            ret_decay_fwd[layer], ret_decay_bwd[layer], qkvg_r,
            ret_norm_w[layer].reshape(N_HEADS, 1, HEAD_DIM), w_out[layer], w_down[layer])
        hid, n2 = _outproj(attn, ret, hid, w_out_l, norm_ffn_w[layer].reshape(1, d_model))
        last = layer == depth - 1
        hid = _ffn(n2, hid, w_gate_l, w_up_l, w_down_l, final_w, final_norm=last)
    return hid.reshape(batch, seq, d_model)
```

```python
import functools

import jax
import jax.numpy as jnp
from jax import lax
from jax.experimental import pallas as pl
from jax.experimental.pallas import tpu as pltpu

F32 = jnp.float32
BF16 = jnp.bfloat16

HEAD_DIM = 128
N_HEADS = 8
D_GROUP = N_HEADS * HEAD_DIM
DILATED_PATTERNS = ((128, 1), (512, 4), (2048, 16))
HALF_BAND = 64
RET_BLOCK = 256
EPS = 1e-6
NEG_BIG = -1e30
LOG2_E = 1.4426950408889634
MIB = 1024 * 1024

FFN_TF = 512
ATT_TQ = 128
ATT_TK = ATT_TQ + 2 * HALF_BAND


def _params(semantics, vmem_mib, flags=None):
    return pltpu.CompilerParams(dimension_semantics=semantics,
                                vmem_limit_bytes=vmem_mib * MIB, flags=flags)


def _to_bf16(x, mul=None):
    return (x if mul is None else x * mul).astype(BF16)


def _rmsnorm_rows(x, w):
    return x * lax.rsqrt(jnp.mean(x * x, axis=-1, keepdims=True) + EPS) * w


def _inproj_kernel(x_ref, nw_ref, w_ref, *rest, groups, groups_a, x_slices,
                   n_carried, emit_weight):
    rest = rest[n_carried:]
    oa_ref, ob_ref = rest[0], rest[1]
    wb_ref = rest[2] if emit_weight else None
    n_scr = rest[-1]
    i = pl.program_id(0)
    j = pl.program_id(1)
    slice_rows = x_ref.shape[0]

    def norm_slice():
        rows = pl.ds(pl.multiple_of(j * slice_rows, slice_rows), slice_rows)
        n_scr[i & 1, rows, :] = _rmsnorm_rows(x_ref[...], nw_ref[...]).astype(BF16)

    def project(o_ref, with_norm):
        w = w_ref[...].astype(BF16)
        if emit_weight:
            wb_ref[...] = w
        res = jnp.dot(n_scr[(i - 1) & 1], w, preferred_element_type=F32)
        for h in range(N_HEADS):
            o_ref[h] = res[:, h * HEAD_DIM:(h + 1) * HEAD_DIM].astype(o_ref.dtype)
        if with_norm:
            norm_slice()

    @pl.when(j == 0)
    def _():
        ob_ref[...] = jnp.zeros_like(ob_ref)

    @pl.when((i == 0) & (j == 0))
    def _():
        oa_ref[...] = jnp.zeros_like(oa_ref)
        if emit_weight:
            wb_ref[...] = jnp.zeros_like(wb_ref)

    pl.when((i == 0) & (j < x_slices))(norm_slice)
    cases = {}
    for step in range(groups):
        cases.setdefault((step < groups_a, step < x_slices), []).append(step)
    for (to_a, with_norm), steps in cases.items():
        pl.when((i > 0) & (j >= steps[0]) & (j <= steps[-1]))(
            functools.partial(project, oa_ref if to_a else ob_ref, with_norm))


def _inproj_tiles(x2, norm_w, w, carried, first_tile, n_tiles, groups_a, dtype_a, dtype_b,
                  batch, seq, tm, x_slices):
    n_tok, d_model = x2.shape
    groups = w.shape[1] // D_GROUP
    groups_b = groups - groups_a
    s_blocks = seq // tm
    emit_weight = w.dtype != BF16
    assert x_slices <= groups and tm % x_slices == 0

    def out_spec(group_of_j):
        def index(i, j):
            t = first_tile + jnp.maximum(i - 1, 0)
            g = jnp.where(i == 0, 0, group_of_j(j))
            return (g, t // s_blocks, 0, t % s_blocks, 0)
        return pl.BlockSpec((None, None, N_HEADS, tm, HEAD_DIM), index)

    def out_shape(n_groups, dtype):
        return jax.ShapeDtypeStruct((n_groups, batch, N_HEADS, seq, HEAD_DIM), dtype)

    w_spec = pl.BlockSpec((d_model, D_GROUP), lambda i, j: (0, jnp.where(i == 0, 0, j)))
    out_specs = [out_spec(lambda j: jnp.minimum(j, groups_a - 1)),
                 out_spec(lambda j: jnp.maximum(j - groups_a, 0))]
    out_shapes = [out_shape(groups_a, dtype_a), out_shape(groups_b, dtype_b)]
    if emit_weight:
        out_specs.append(w_spec)
        out_shapes.append(jax.ShapeDtypeStruct(w.shape, BF16))
    return pl.pallas_call(
        functools.partial(_inproj_kernel, groups=groups, groups_a=groups_a,
                          x_slices=x_slices, n_carried=len(carried),
                          emit_weight=emit_weight),
        grid=(n_tiles + 1, groups),
        in_specs=[
            pl.BlockSpec((tm // x_slices, d_model),
                         lambda i, j: ((first_tile + jnp.minimum(i, n_tiles - 1)) * x_slices
                                       + jnp.minimum(j, x_slices - 1), 0)),
            pl.BlockSpec((1, d_model), lambda i, j: (0, 0)),
            w_spec,
        ] + [pl.BlockSpec(memory_space=pl.ANY)] * len(carried),
        out_specs=out_specs,
        out_shape=out_shapes,
        input_output_aliases={3 + k: k for k in range(len(carried))},
        scratch_shapes=[pltpu.VMEM((2, tm, d_model), BF16)],
        compiler_params=_params(("arbitrary", "arbitrary"), 56),
        name="inproj",
    )(x2, norm_w, w, *carried)


def _inproj(x2, norm_w, w_f32, groups_a, dtype_a, dtype_b, batch, seq, tm=1024, x_slices=4):
    n_tiles = x2.shape[0] // tm
    args = (groups_a, dtype_a, dtype_b, batch, seq, tm, x_slices)
    oa, ob, w_bf = _inproj_tiles(x2, norm_w, w_f32, (), 0, 1, *args)
    return _inproj_tiles(x2, norm_w, w_bf, (oa, ob), 1, n_tiles - 1, *args)


def _attn_kernel(slopes_ref, q_ref, k_ref, v_ref, wg_ref, wu_ref,
                 o_ref, wg_bf, wu_bf,
                 q4f, k4f, v4f,
                 num4, den4, m4, num1, den1, m1, bias_s):
    for w_src, w_dst in ((wg_ref, wg_bf), (wu_ref, wu_bf)):
        w_dst[...] = w_src[...].astype(BF16)

    seq = q_ref.shape[0]
    quarter = seq // 4
    q_scale = HEAD_DIM ** -0.5 * LOG2_E

    @pl.when(pl.program_id(1) == 0)
    def _():
        slope = slopes_ref[pl.program_id(0)]
        row = lax.broadcasted_iota(jnp.int32, (ATT_TQ, ATT_TK), 0)
        col = lax.broadcasted_iota(jnp.int32, (ATT_TQ, ATT_TK), 1)
        for p, (_, dil) in enumerate(DILATED_PATTERNS):
            for c, off in enumerate((0, HALF_BAND, 2 * HALF_BAND)):
                dist = jnp.abs(row - col + off)
                bias = -slope * (dil * dist).astype(F32) * LOG2_E
                bias_s[p * 3 + c] = jnp.where(dist <= HALF_BAND, bias, NEG_BIG)

    for src, dst in ((q_ref, q4f), (k_ref, k4f), (v_ref, v4f)):
        for r in range(4):
            dst[r * quarter:(r + 1) * quarter, :] = src[pl.ds(r, quarter, stride=4), :]

    def run_pattern(p, dil, sources, rows_of, incoming, emit):
        q_src, k_src, v_src = sources
        sub_len = seq // dil
        n_blk = sub_len // ATT_TQ
        shift = n_blk.bit_length() - 1
        assert n_blk == 1 << shift and n_blk >= 2

        def block(idx, carry):
            r = idx >> shift
            blk = idx & (n_blk - 1)
            q0 = blk * ATT_TQ
            k0 = jnp.clip(q0 - HALF_BAND, 0, sub_len - ATT_TK)
            q_rows = pl.ds(pl.multiple_of(r * sub_len + q0, ATT_TQ), ATT_TQ)
            case = jnp.where(blk == 0, 0, jnp.where(blk == n_blk - 1, 2, 1))
            q = _to_bf16(q_src[rows_of(r, q0, ATT_TQ), :], q_scale)
            k = _to_bf16(k_src[rows_of(r, k0, ATT_TK), :])
            v = _to_bf16(v_src[rows_of(r, k0, ATT_TK), :])
            s = lax.dot_general(q, k, (((1,), (1,)), ((), ())),
                                preferred_element_type=F32)
            s = s + bias_s[p * 3 + case]
            m_blk = jnp.max(s, axis=-1, keepdims=True)
            if incoming is None:
                m_new = jnp.broadcast_to(m_blk, (ATT_TQ, HEAD_DIM))
            else:
                num_in, den_in, m_in = incoming
                m_old = m_in[q_rows, :]
                m_new = jnp.maximum(m_old, m_blk)
            e = jnp.exp2(s - jnp.concatenate([m_new, m_new], axis=-1))
            pv = jnp.dot(e.astype(BF16), jnp.concatenate([v, jnp.ones_like(v)], axis=-1),
                         preferred_element_type=F32)
            num = pv[:, :HEAD_DIM]
            den = pv[:, HEAD_DIM:]
            if incoming is not None:
                alpha = jnp.exp2(m_old - m_new)
                num = alpha * num_in[q_rows, :] + num
                den = alpha * den_in[q_rows, :] + den
            emit(r, q0, num, den, m_new)
            return carry

        lax.fori_loop(0, dil * n_blk, block, 0, unroll=32)

    copies4 = (q4f, k4f, v4f)

    def rows16(r, l0, n):
        return pl.ds((r & 3) * quarter + 4 * l0 + (r >> 2), n, stride=4)

    def emit16(r, q0, num, den, m_new):
        rows = pl.ds((r & 3) * quarter + 4 * q0 + (r >> 2), ATT_TQ, stride=4)
        num4[rows, :] = num
        den4[rows, :] = den
        m4[rows, :] = m_new

    run_pattern(2, 16, copies4, rows16, None, emit16)

    def rows4(r, l0, n):
        return pl.ds(pl.multiple_of(r * quarter + l0, HALF_BAND), n)

    def emit4(r, q0, num, den, m_new):
        rows = pl.ds(4 * q0 + r, ATT_TQ, stride=4)
        num1[rows, :] = num
        den1[rows, :] = den
        m1[rows, :] = m_new

    run_pattern(1, 4, copies4, rows4, (num4, den4, m4), emit4)

    def rows1(r, l0, n):
        return pl.ds(pl.multiple_of(l0, HALF_BAND), n)

    def emit1(r, q0, num, den, m_new):
        o_ref[pl.ds(pl.multiple_of(q0, ATT_TQ), ATT_TQ), :] = (num / den).astype(o_ref.dtype)

    run_pattern(0, 1, (q_ref, k_ref, v_ref), rows1, (num1, den1, m1), emit1)


def _cast_specs(weights, n_steps, step_of):
    specs, shapes = [], []
    for w in weights:
        rows = w.shape[0] // n_steps
        assert rows * n_steps == w.shape[0] and rows % 16 == 0
        specs.append(pl.BlockSpec((rows, w.shape[1]), lambda *idx: (step_of(*idx), 0)))
        shapes.append(jax.ShapeDtypeStruct(w.shape, BF16))
    return specs, shapes


def _attention(slopes, qkv, ffn_weights):
    _, batch, heads, seq, dh = qkv.shape
    assert seq % (16 * ATT_TK) == 0 and dh == HEAD_DIM

    def spec(t):
        return pl.BlockSpec((None, None, None, seq, dh),
                            lambda h, b, t=t: (t, b, h, 0, 0))

    w_specs, w_shapes = _cast_specs(ffn_weights, heads * batch, lambda h, b: h * batch + b)
    f32_rows = pltpu.VMEM((seq, dh), F32)
    return pl.pallas_call(
        _attn_kernel,
        grid=(heads, batch),
        in_specs=[pl.BlockSpec(memory_space=pltpu.SMEM), spec(0), spec(1), spec(2)] + w_specs,
        out_specs=[pl.BlockSpec((None, None, seq, dh), lambda h, b: (b, h, 0, 0))] + w_specs,
        out_shape=[jax.ShapeDtypeStruct((batch, heads, seq, dh), BF16)] + w_shapes,
        scratch_shapes=[
            f32_rows, f32_rows, f32_rows,
            f32_rows, f32_rows, f32_rows, f32_rows, f32_rows, f32_rows,
            pltpu.VMEM((3 * len(DILATED_PATTERNS), ATT_TQ, ATT_TK), F32),
        ],
        compiler_params=_params(("parallel", "arbitrary"), 56),
        name="dilated_attention",
    )(slopes, qkv, qkv, qkv, *ffn_weights)


def _ret_kernel(df_ref, db_ref, q_ref, k_ref, v_ref, g_ref, rw_ref, wo_ref, wd_ref,
                o_ref, wo_bf, wd_bf, intra_s, tab, kv_s, st_s):
    wo_bf[...] = wo_ref[...].astype(BF16)
    wd_bf[...] = wd_ref[...].astype(BF16)

    seq = q_ref.shape[0]
    ch = RET_BLOCK
    n_ch = seq // ch
    h = pl.program_id(0)
    scale = HEAD_DIM ** -0.5
    T_QF, T_QB, T_KF, T_KB = range(4)

    lg_f = -jnp.exp(jnp.full((1, HEAD_DIM), df_ref[h], F32))
    lg_b = -jnp.exp(jnp.full((1, HEAD_DIM), db_ref[h], F32))

    @pl.when(pl.program_id(1) == 0)
    def _():
        rel = (lax.broadcasted_iota(jnp.int32, (ch, ch), 0)
               - lax.broadcasted_iota(jnp.int32, (ch, ch), 1)).astype(F32)
        intra_s[...] = jnp.where(rel >= 0.0,
                                 jnp.exp(lg_f[:, :1] * jnp.maximum(rel, 0.0)),
                                 jnp.exp(lg_b[:, :1] * jnp.maximum(-rel, 0.0))) * scale
        ri = lax.broadcasted_iota(jnp.int32, (ch, HEAD_DIM), 0).astype(F32)
        tab[T_QF] = jnp.exp(lg_f * (ri + 1.0)) * scale
        tab[T_QB] = jnp.exp(lg_b * (ch - ri)) * scale
        tab[T_KF] = jnp.exp(lg_f * (ch - 1.0 - ri))
        tab[T_KB] = jnp.exp(lg_b * ri)

    cdec_f = jnp.exp(lg_f * ch)
    cdec_b = jnp.exp(lg_b * ch)

    def chunk_rows(c):
        return pl.ds(pl.multiple_of(c * ch, ch), ch)

    def kv_step(c, carry):
        rows = chunk_rows(c)
        vc = v_ref[rows, :].astype(F32)
        v2 = jnp.concatenate([vc * tab[T_KF], vc * tab[T_KB]], axis=-1).astype(BF16)
        kv_s[c] = lax.dot_general(k_ref[rows, :], v2, (((0,), (0,)), ((), ())),
                                  preferred_element_type=F32)
        return carry

    lax.fori_loop(0, n_ch, kv_step, 0, unroll=n_ch)
    zero_state = jnp.zeros((HEAD_DIM, HEAD_DIM), F32)

    def scan_f(c, state):
        st_s[c, :, :HEAD_DIM] = state.astype(BF16)
        return state * cdec_f + kv_s[c, :, :HEAD_DIM]

    def scan_b(i, state):
        c = n_ch - 1 - i
        st_s[c, :, HEAD_DIM:] = state.astype(BF16)
        return state * cdec_b + kv_s[c, :, HEAD_DIM:]

    lax.fori_loop(0, n_ch, scan_f, zero_state, unroll=n_ch)
    lax.fori_loop(0, n_ch, scan_b, zero_state, unroll=n_ch)

    def out_step(c, carry):
        rows = chunk_rows(c)
        qb = q_ref[rows, :]
        inner = lax.dot_general(qb, k_ref[rows, :], (((1,), (1,)), ((), ())),
                                preferred_element_type=F32) * intra_s[...]
        o = jnp.dot(inner.astype(BF16), v_ref[rows, :], preferred_element_type=F32)
        cross = jnp.dot(qb, st_s[c], preferred_element_type=F32)
        o = o + cross[:, :HEAD_DIM] * tab[T_QF] + cross[:, HEAD_DIM:] * tab[T_QB]
        o = o * lax.rsqrt(jnp.mean(o * o, axis=-1, keepdims=True) + EPS)
        o = o * rw_ref[...]
        g = g_ref[rows, :].astype(F32)
        o = o * (g * (1.0 / (1.0 + jnp.exp(-g))))
        o_ref[rows, :] = o.astype(o_ref.dtype)
        return carry

    lax.fori_loop(0, n_ch, out_step, 0, unroll=n_ch)


def _retention(decay_f, decay_b, qkvg, ret_norm_w, w_out, w_down):
    _, batch, heads, seq, dh = qkvg.shape

    def spec(t):
        return pl.BlockSpec((None, None, None, seq, dh),
                            lambda h, b, t=t: (t, b, h, 0, 0))

    smem = pl.BlockSpec(memory_space=pltpu.SMEM)
    w_specs, w_shapes = _cast_specs([w_out, w_down], heads * batch,
                                    lambda h, b: h * batch + b)
    return pl.pallas_call(
        _ret_kernel,
        grid=(heads, batch),
        in_specs=[smem, smem, spec(0), spec(1), spec(2), spec(3),
                  pl.BlockSpec((None, 1, dh), lambda h, b: (h, 0, 0))] + w_specs,
        out_specs=[pl.BlockSpec((None, None, seq, dh), lambda h, b: (b, h, 0, 0))] + w_specs,
        out_shape=[jax.ShapeDtypeStruct((batch, heads, seq, dh), BF16)] + w_shapes,
        scratch_shapes=[pltpu.VMEM((RET_BLOCK, RET_BLOCK), F32),
                        pltpu.VMEM((4, RET_BLOCK, dh), F32),
                        pltpu.VMEM((seq // RET_BLOCK, dh, 2 * dh), F32),
                        pltpu.VMEM((seq // RET_BLOCK, dh, 2 * dh), BF16)],
        compiler_params=_params(("parallel", "arbitrary"), 32),
        name="retention",
    )(decay_f, decay_b, qkvg, qkvg, qkvg, qkvg, ret_norm_w, w_out, w_down)


def _outproj_kernel(a_ref, r_ref, x_ref, w_ref, nw_ref, h_ref, n_ref):
    mixed = jnp.concatenate([a_ref[h] for h in range(N_HEADS)]
                            + [r_ref[h] for h in range(N_HEADS)], axis=-1)
    hid = x_ref[...] + jnp.dot(mixed, w_ref[...], preferred_element_type=F32)
    h_ref[...] = hid
    n_ref[...] = _rmsnorm_rows(hid, nw_ref[...]).astype(n_ref.dtype)


def _outproj(attn, ret, x2, w_out, norm_w, tm=512):
    batch, heads, seq, dh = attn.shape
    n_tok, d_model = x2.shape
    s_blocks = seq // tm
    head_spec = pl.BlockSpec((None, heads, tm, dh),
                             lambda i: (i // s_blocks, 0, i % s_blocks, 0))
    row_spec = pl.BlockSpec((tm, d_model), lambda i: (i, 0))
    return pl.pallas_call(
        _outproj_kernel,
        grid=(n_tok // tm,),
        in_specs=[head_spec, head_spec, row_spec,
                  pl.BlockSpec(w_out.shape, lambda i: (0, 0)),
                  pl.BlockSpec((1, d_model), lambda i: (0, 0))],
        out_specs=[row_spec, row_spec],
        out_shape=[jax.ShapeDtypeStruct((n_tok, d_model), F32),
                   jax.ShapeDtypeStruct((n_tok, d_model), BF16)],
        compiler_params=_params(("parallel",), 56),
        name="outproj",
    )(attn, ret, x2, w_out, norm_w)


def _ffn_kernel(n_ref, h_ref, wg_ref, wu_ref, wd_ref, fw_ref, o_ref, *,
                final_norm, h_slices):
    f = pl.program_id(1)
    h_rows = h_ref.shape[0]

    @pl.when(f == 0)
    def _():
        o_ref[...] = jnp.zeros_like(o_ref)

    n = n_ref[...]
    g = jnp.dot(n, wg_ref[...], preferred_element_type=F32)
    u = jnp.dot(n, wu_ref[...], preferred_element_type=F32)
    a = (g * (1.0 / (1.0 + jnp.exp(-g))) * u).astype(BF16)
    o_ref[...] += jnp.dot(a, wd_ref[...], preferred_element_type=F32)

    slice_idx = jnp.minimum(f, h_slices - 1)
    rows = pl.ds(pl.multiple_of(slice_idx * h_rows, h_rows), h_rows)
    o_ref[rows, :] += h_ref[...] * (f < h_slices).astype(F32)

    if final_norm:
        @pl.when(f == pl.num_programs(1) - 1)
        def _():
            o_ref[...] = _rmsnorm_rows(o_ref[...], fw_ref[...])


def _ffn(n2, hid, w_gate, w_up, w_down, final_w, final_norm, tm=1024, tf=FFN_TF,
         h_slices=8):
    n_tok, d_model = hid.shape
    f_steps = w_gate.shape[1] // tf
    assert f_steps >= h_slices and tm % h_slices == 0
    row = lambda i, f: (i, 0)
    col_tile = pl.BlockSpec((d_model, tf), lambda i, f: (0, f))
    return pl.pallas_call(
        functools.partial(_ffn_kernel, final_norm=final_norm, h_slices=h_slices),
        grid=(n_tok // tm, f_steps),
        in_specs=[pl.BlockSpec((tm, d_model), row),
                  pl.BlockSpec((tm // h_slices, d_model),
                               lambda i, f: (i * h_slices + jnp.minimum(f, h_slices - 1), 0)),
                  col_tile, col_tile,
                  pl.BlockSpec((tf, d_model), lambda i, f: (f, 0)),
                  pl.BlockSpec((1, d_model), lambda i, f: (0, 0))],
        out_specs=pl.BlockSpec((tm, d_model), row),
        out_shape=jax.ShapeDtypeStruct((n_tok, d_model), F32),
        compiler_params=_params(("parallel", "arbitrary"), 56),
        name="ffn",
    )(n2, hid, w_gate, w_up, w_down, final_w)


def kernel(x, norm_mix_w, w_in, ret_decay_fwd, ret_decay_bwd, ret_norm_w, w_out,
           norm_ffn_w, w_gate, w_up, w_down, norm_final_w):
    batch, seq, d_model = x.shape
    depth = w_in.shape[0]
    assert depth >= 1
    slopes = jnp.exp2(-8.0 * jnp.arange(1, N_HEADS + 1, dtype=F32) / N_HEADS)
    final_w = norm_final_w.reshape(1, d_model)

    hid = x.reshape(batch * seq, d_model)
    for layer in range(depth):
        nw = norm_mix_w[layer].reshape(1, d_model)
        qkv_a, qkvg_r = _inproj(hid, nw, w_in[layer], 3, F32, BF16, batch, seq)
        attn, w_gate_l, w_up_l = _attention(slopes, qkv_a, [w_gate[layer], w_up[layer]])
        ret, w_out_l, w_down_l = _retention(
            ret_decay_fwd[layer], ret_decay_bwd[layer], qkvg_r,
            ret_norm_w[layer].reshape(N_HEADS, 1, HEAD_DIM), w_out[layer], w_down[layer])
        hid, n2 = _outproj(attn, ret, hid, w_out_l, norm_ffn_w[layer].reshape(1, d_model))
        last = layer == depth - 1
        hid = _ffn(n2, hid, w_gate_l, w_up_l, w_down_l, final_w, final_norm=last)
    return hid.reshape(batch, seq, d_model)
```

```python
import functools

import jax
import jax.numpy as jnp
from jax import lax
from jax.experimental import pallas as pl
from jax.experimental.pallas import tpu as pltpu

F32 = jnp.float32
BF16 = jnp.bfloat16

HEAD_DIM = 128
N_HEADS = 8
D_GROUP = N_HEADS * HEAD_DIM
DILATED_PATTERNS = ((128, 1), (512, 4), (2048, 16))
HALF_BAND = 64
RET_BLOCK = 256
EPS = 1e-6
NEG_BIG = -1e30
LOG2_E = 1.4426950408889634
MIB = 1024 * 1024

FFN_TF = 512
ATT_TQ = 128
ATT_TK = ATT_TQ + 2 * HALF_BAND


def _params(semantics, vmem_mib):
    return pltpu.CompilerParams(dimension_semantics=semantics,
                                vmem_limit_bytes=vmem_mib * MIB)


def _to_bf16(x, mul=None):
    return (x if mul is None else x * mul).astype(BF16)


def _rmsnorm_rows(x, w):
    return x * lax.rsqrt(jnp.mean(x * x, axis=-1, keepdims=True) + EPS) * w


def _inproj_kernel(x_ref, nw_ref, w_ref, *rest, groups, groups_a, x_slices,
                   n_carried, emit_weight):
    rest = rest[n_carried:]
    oa_ref, ob_ref = rest[0], rest[1]
    wb_ref = rest[2] if emit_weight else None
    n_scr = rest[-1]
    i = pl.program_id(0)
    j = pl.program_id(1)
    slice_rows = x_ref.shape[0]

    def norm_slice():
        rows = pl.ds(pl.multiple_of(j * slice_rows, slice_rows), slice_rows)
        n_scr[i & 1, rows, :] = _rmsnorm_rows(x_ref[...], nw_ref[...]).astype(BF16)

    def project(o_ref, with_norm):
        w = w_ref[...].astype(BF16)
        if emit_weight:
            wb_ref[...] = w
        res = jnp.dot(n_scr[(i - 1) & 1], w, preferred_element_type=F32)
        for h in range(N_HEADS):
            o_ref[h] = res[:, h * HEAD_DIM:(h + 1) * HEAD_DIM].astype(o_ref.dtype)
        if with_norm:
            norm_slice()

    @pl.when(j == 0)
    def _():
        ob_ref[...] = jnp.zeros_like(ob_ref)

    @pl.when((i == 0) & (j == 0))
    def _():
        oa_ref[...] = jnp.zeros_like(oa_ref)
        if emit_weight:
            wb_ref[...] = jnp.zeros_like(wb_ref)

    pl.when((i == 0) & (j < x_slices))(norm_slice)
    cases = {}
    for step in range(groups):
        cases.setdefault((step < groups_a, step < x_slices), []).append(step)
    for (to_a, with_norm), steps in cases.items():
        pl.when((i > 0) & (j >= steps[0]) & (j <= steps[-1]))(
            functools.partial(project, oa_ref if to_a else ob_ref, with_norm))


def _inproj_tiles(x2, norm_w, w, carried, first_tile, n_tiles, groups_a, dtype_a, dtype_b,
                  batch, seq, tm, x_slices):
    n_tok, d_model = x2.shape
    groups = w.shape[1] // D_GROUP
    groups_b = groups - groups_a
    s_blocks = seq // tm
    emit_weight = w.dtype != BF16
    assert x_slices <= groups and tm % x_slices == 0

    def out_spec(group_of_j):
        def index(i, j):
            t = first_tile + jnp.maximum(i - 1, 0)
            g = jnp.where(i == 0, 0, group_of_j(j))
            return (g, t // s_blocks, 0, t % s_blocks, 0)
        return pl.BlockSpec((None, None, N_HEADS, tm, HEAD_DIM), index)

    def out_shape(n_groups, dtype):
        return jax.ShapeDtypeStruct((n_groups, batch, N_HEADS, seq, HEAD_DIM), dtype)

    w_spec = pl.BlockSpec((d_model, D_GROUP), lambda i, j: (0, jnp.where(i == 0, 0, j)))
    out_specs = [out_spec(lambda j: jnp.minimum(j, groups_a - 1)),
                 out_spec(lambda j: jnp.maximum(j - groups_a, 0))]
    out_shapes = [out_shape(groups_a, dtype_a), out_shape(groups_b, dtype_b)]
    if emit_weight:
        out_specs.append(w_spec)
        out_shapes.append(jax.ShapeDtypeStruct(w.shape, BF16))
    return pl.pallas_call(
        functools.partial(_inproj_kernel, groups=groups, groups_a=groups_a,
                          x_slices=x_slices, n_carried=len(carried),
                          emit_weight=emit_weight),
        grid=(n_tiles + 1, groups),
        in_specs=[
            pl.BlockSpec((tm // x_slices, d_model),
                         lambda i, j: ((first_tile + jnp.minimum(i, n_tiles - 1)) * x_slices
                                       + jnp.minimum(j, x_slices - 1), 0)),
            pl.BlockSpec((1, d_model), lambda i, j: (0, 0)),
            w_spec,
        ] + [pl.BlockSpec(memory_space=pl.ANY)] * len(carried),
        out_specs=out_specs,
        out_shape=out_shapes,
        input_output_aliases={3 + k: k for k in range(len(carried))},
        scratch_shapes=[pltpu.VMEM((2, tm, d_model), BF16)],
        compiler_params=_params(("arbitrary", "arbitrary"), 56),
        name="inproj",
    )(x2, norm_w, w, *carried)


def _inproj(x2, norm_w, w_f32, groups_a, dtype_a, dtype_b, batch, seq, tm=1024, x_slices=4):
    n_tiles = x2.shape[0] // tm
    args = (groups_a, dtype_a, dtype_b, batch, seq, tm, x_slices)
    oa, ob, w_bf = _inproj_tiles(x2, norm_w, w_f32, (), 0, 1, *args)
    return _inproj_tiles(x2, norm_w, w_bf, (oa, ob), 1, n_tiles - 1, *args)


def _attn_kernel(slopes_ref, q_ref, k_ref, v_ref, wg_ref, wu_ref,
                 o_ref, wg_bf, wu_bf,
                 q4f, k4f, v4f,
                 num4, den4, m4, num1, den1, m1, bias_s):
    for w_src, w_dst in ((wg_ref, wg_bf), (wu_ref, wu_bf)):
        w_dst[...] = w_src[...].astype(BF16)

    seq = q_ref.shape[0]
    quarter = seq // 4
    q_scale = HEAD_DIM ** -0.5 * LOG2_E

    @pl.when(pl.program_id(1) == 0)
    def _():
        slope = slopes_ref[pl.program_id(0)]
        row = lax.broadcasted_iota(jnp.int32, (ATT_TQ, ATT_TK), 0)
        col = lax.broadcasted_iota(jnp.int32, (ATT_TQ, ATT_TK), 1)
        for p, (_, dil) in enumerate(DILATED_PATTERNS):
            for c, off in enumerate((0, HALF_BAND, 2 * HALF_BAND)):
                dist = jnp.abs(row - col + off)
                bias = -slope * (dil * dist).astype(F32) * LOG2_E
                bias_s[p * 3 + c] = jnp.where(dist <= HALF_BAND, bias, NEG_BIG)

    for src, dst in ((q_ref, q4f), (k_ref, k4f), (v_ref, v4f)):
        for r in range(4):
            dst[r * quarter:(r + 1) * quarter, :] = src[pl.ds(r, quarter, stride=4), :]

    def run_pattern(p, dil, sources, rows_of, incoming, emit):
        q_src, k_src, v_src = sources
        sub_len = seq // dil
        n_blk = sub_len // ATT_TQ
        shift = n_blk.bit_length() - 1
        assert n_blk == 1 << shift and n_blk >= 2

        def block(idx, carry):
            r = idx >> shift
            blk = idx & (n_blk - 1)
            q0 = blk * ATT_TQ
            k0 = jnp.clip(q0 - HALF_BAND, 0, sub_len - ATT_TK)
            q_rows = pl.ds(pl.multiple_of(r * sub_len + q0, ATT_TQ), ATT_TQ)
            case = jnp.where(blk == 0, 0, jnp.where(blk == n_blk - 1, 2, 1))
            q = _to_bf16(q_src[rows_of(r, q0, ATT_TQ), :], q_scale)
            k = _to_bf16(k_src[rows_of(r, k0, ATT_TK), :])
            v = _to_bf16(v_src[rows_of(r, k0, ATT_TK), :])
            s = lax.dot_general(q, k, (((1,), (1,)), ((), ())),
                                preferred_element_type=F32)
            s = s + bias_s[p * 3 + case]
            m_blk = jnp.max(s, axis=-1, keepdims=True)
            if incoming is None:
                m_new = jnp.broadcast_to(m_blk, (ATT_TQ, HEAD_DIM))
            else:
                num_in, den_in, m_in = incoming
                m_old = m_in[q_rows, :]
                m_new = jnp.maximum(m_old, m_blk)
            e = jnp.exp2(s - jnp.concatenate([m_new, m_new], axis=-1))
            pv = jnp.dot(e.astype(BF16), jnp.concatenate([v, jnp.ones_like(v)], axis=-1),
                         preferred_element_type=F32)
            num = pv[:, :HEAD_DIM]
            den = pv[:, HEAD_DIM:]
            if incoming is not None:
                alpha = jnp.exp2(m_old - m_new)
                num = alpha * num_in[q_rows, :] + num
                den = alpha * den_in[q_rows, :] + den
            emit(r, q0, num, den, m_new)
            return carry

        lax.fori_loop(0, dil * n_blk, block, 0, unroll=32)

    copies4 = (q4f, k4f, v4f)

    def rows16(r, l0, n):
        return pl.ds((r & 3) * quarter + 4 * l0 + (r >> 2), n, stride=4)

    def emit16(r, q0, num, den, m_new):
        rows = pl.ds((r & 3) * quarter + 4 * q0 + (r >> 2), ATT_TQ, stride=4)
        num4[rows, :] = num
        den4[rows, :] = den
        m4[rows, :] = m_new

    run_pattern(2, 16, copies4, rows16, None, emit16)

    def rows4(r, l0, n):
        return pl.ds(pl.multiple_of(r * quarter + l0, HALF_BAND), n)

    def emit4(r, q0, num, den, m_new):
        rows = pl.ds(4 * q0 + r, ATT_TQ, stride=4)
        num1[rows, :] = num
        den1[rows, :] = den
        m1[rows, :] = m_new

    run_pattern(1, 4, copies4, rows4, (num4, den4, m4), emit4)

    def rows1(r, l0, n):
        return pl.ds(pl.multiple_of(l0, HALF_BAND), n)

    def emit1(r, q0, num, den, m_new):
        o_ref[pl.ds(pl.multiple_of(q0, ATT_TQ), ATT_TQ), :] = (num / den).astype(o_ref.dtype)

    run_pattern(0, 1, (q_ref, k_ref, v_ref), rows1, (num1, den1, m1), emit1)


def _cast_specs(weights, n_steps, step_of):
    specs, shapes = [], []
    for w in weights:
        rows = w.shape[0] // n_steps
        assert rows * n_steps == w.shape[0] and rows % 16 == 0
        specs.append(pl.BlockSpec((rows, w.shape[1]), lambda *idx: (step_of(*idx), 0)))
        shapes.append(jax.ShapeDtypeStruct(w.shape, BF16))
    return specs, shapes


def _attention(slopes, qkv, ffn_weights):
    _, batch, heads, seq, dh = qkv.shape
    assert seq % (16 * ATT_TK) == 0 and dh == HEAD_DIM

    def spec(t):
        return pl.BlockSpec((None, None, None, seq, dh),
                            lambda h, b, t=t: (t, b, h, 0, 0))

    w_specs, w_shapes = _cast_specs(ffn_weights, heads * batch, lambda h, b: h * batch + b)
    f32_rows = pltpu.VMEM((seq, dh), F32)
    return pl.pallas_call(
        _attn_kernel,
        grid=(heads, batch),
        in_specs=[pl.BlockSpec(memory_space=pltpu.SMEM), spec(0), spec(1), spec(2)] + w_specs,
        out_specs=[pl.BlockSpec((None, None, seq, dh), lambda h, b: (b, h, 0, 0))] + w_specs,
        out_shape=[jax.ShapeDtypeStruct((batch, heads, seq, dh), BF16)] + w_shapes,
        scratch_shapes=[
            f32_rows, f32_rows, f32_rows,
            f32_rows, f32_rows, f32_rows, f32_rows, f32_rows, f32_rows,
            pltpu.VMEM((3 * len(DILATED_PATTERNS), ATT_TQ, ATT_TK), F32),
        ],
        compiler_params=_params(("parallel", "arbitrary"), 56),
        name="dilated_attention",
    )(slopes, qkv, qkv, qkv, *ffn_weights)


def _ret_kernel(df_ref, db_ref, q_ref, k_ref, v_ref, g_ref, rw_ref, wo_ref, wd_ref,
                o_ref, wo_bf, wd_bf, intra_s, tab, kv_s, st_s):
    wo_bf[...] = wo_ref[...].astype(BF16)
    wd_bf[...] = wd_ref[...].astype(BF16)

    seq = q_ref.shape[0]
    ch = RET_BLOCK
    n_ch = seq // ch
    h = pl.program_id(0)
    scale = HEAD_DIM ** -0.5
    T_QF, T_QB, T_KF, T_KB = range(4)

    lg_f = -jnp.exp(jnp.full((1, HEAD_DIM), df_ref[h], F32))
    lg_b = -jnp.exp(jnp.full((1, HEAD_DIM), db_ref[h], F32))

    @pl.when(pl.program_id(1) == 0)
    def _():
        rel = (lax.broadcasted_iota(jnp.int32, (ch, ch), 0)
               - lax.broadcasted_iota(jnp.int32, (ch, ch), 1)).astype(F32)
        intra_s[...] = jnp.where(rel >= 0.0,
                                 jnp.exp(lg_f[:, :1] * jnp.maximum(rel, 0.0)),
                                 jnp.exp(lg_b[:, :1] * jnp.maximum(-rel, 0.0))) * scale
        ri = lax.broadcasted_iota(jnp.int32, (ch, HEAD_DIM), 0).astype(F32)
        tab[T_QF] = jnp.exp(lg_f * (ri + 1.0)) * scale
        tab[T_QB] = jnp.exp(lg_b * (ch - ri)) * scale
        tab[T_KF] = jnp.exp(lg_f * (ch - 1.0 - ri))
        tab[T_KB] = jnp.exp(lg_b * ri)

    cdec_f = jnp.exp(lg_f * ch)
    cdec_b = jnp.exp(lg_b * ch)

    def chunk_rows(c):
        return pl.ds(pl.multiple_of(c * ch, ch), ch)

    def kv_step(c, carry):
        rows = chunk_rows(c)
        vc = v_ref[rows, :].astype(F32)
        v2 = jnp.concatenate([vc * tab[T_KF], vc * tab[T_KB]], axis=-1).astype(BF16)
        kv_s[c] = lax.dot_general(k_ref[rows, :], v2, (((0,), (0,)), ((), ())),
                                  preferred_element_type=F32)
        return carry

    lax.fori_loop(0, n_ch, kv_step, 0, unroll=n_ch)
    zero_state = jnp.zeros((HEAD_DIM, HEAD_DIM), F32)

    def scan_f(c, state):
        st_s[c, :, :HEAD_DIM] = state.astype(BF16)
        return state * cdec_f + kv_s[c, :, :HEAD_DIM]

    def scan_b(i, state):
        c = n_ch - 1 - i
        st_s[c, :, HEAD_DIM:] = state.astype(BF16)
        return state * cdec_b + kv_s[c, :, HEAD_DIM:]

    lax.fori_loop(0, n_ch, scan_f, zero_state, unroll=n_ch)
    lax.fori_loop(0, n_ch, scan_b, zero_state, unroll=n_ch)

    def out_step(c, carry):
        rows = chunk_rows(c)
        qb = q_ref[rows, :]
        inner = lax.dot_general(qb, k_ref[rows, :], (((1,), (1,)), ((), ())),
                                preferred_element_type=F32) * intra_s[...]
        o = jnp.dot(inner.astype(BF16), v_ref[rows, :], preferred_element_type=F32)
        cross = jnp.dot(qb, st_s[c], preferred_element_type=F32)
        o = o + cross[:, :HEAD_DIM] * tab[T_QF] + cross[:, HEAD_DIM:] * tab[T_QB]
        o = o * lax.rsqrt(jnp.mean(o * o, axis=-1, keepdims=True) + EPS)
        o = o * rw_ref[...]
        g = g_ref[rows, :].astype(F32)
        o = o * (g * (1.0 / (1.0 + jnp.exp(-g))))
        o_ref[rows, :] = o.astype(o_ref.dtype)
        return carry

    lax.fori_loop(0, n_ch, out_step, 0, unroll=n_ch)


def _retention(decay_f, decay_b, qkvg, ret_norm_w, w_out, w_down):
    _, batch, heads, seq, dh = qkvg.shape

    def spec(t):
        return pl.BlockSpec((None, None, None, seq, dh),
                            lambda h, b, t=t: (t, b, h, 0, 0))

    smem = pl.BlockSpec(memory_space=pltpu.SMEM)
    w_specs, w_shapes = _cast_specs([w_out, w_down], heads * batch,
                                    lambda h, b: h * batch + b)
    return pl.pallas_call(
        _ret_kernel,
        grid=(heads, batch),
        in_specs=[smem, smem, spec(0), spec(1), spec(2), spec(3),
                  pl.BlockSpec((None, 1, dh), lambda h, b: (h, 0, 0))] + w_specs,
        out_specs=[pl.BlockSpec((None, None, seq, dh), lambda h, b: (b, h, 0, 0))] + w_specs,
        out_shape=[jax.ShapeDtypeStruct((batch, heads, seq, dh), BF16)] + w_shapes,
        scratch_shapes=[pltpu.VMEM((RET_BLOCK, RET_BLOCK), F32),
                        pltpu.VMEM((4, RET_BLOCK, dh), F32),
                        pltpu.VMEM((seq // RET_BLOCK, dh, 2 * dh), F32),
                        pltpu.VMEM((seq // RET_BLOCK, dh, 2 * dh), BF16)],
        compiler_params=_params(("parallel", "arbitrary"), 32),
        name="retention",
    )(decay_f, decay_b, qkvg, qkvg, qkvg, qkvg, ret_norm_w, w_out, w_down)


def _outproj_kernel(a_ref, r_ref, x_ref, w_ref, nw_ref, h_ref, n_ref):
    mixed = jnp.concatenate([a_ref[h] for h in range(N_HEADS)]
                            + [r_ref[h] for h in range(N_HEADS)], axis=-1)
    hid = x_ref[...] + jnp.dot(mixed, w_ref[...], preferred_element_type=F32)
    h_ref[...] = hid
    n_ref[...] = _rmsnorm_rows(hid, nw_ref[...]).astype(n_ref.dtype)


def _outproj(attn, ret, x2, w_out, norm_w, tm=512):
    batch, heads, seq, dh = attn.shape
    n_tok, d_model = x2.shape
    s_blocks = seq // tm
    head_spec = pl.BlockSpec((None, heads, tm, dh),
                             lambda i: (i // s_blocks, 0, i % s_blocks, 0))
    row_spec = pl.BlockSpec((tm, d_model), lambda i: (i, 0))
    return pl.pallas_call(
        _outproj_kernel,
        grid=(n_tok // tm,),
        in_specs=[head_spec, head_spec, row_spec,
                  pl.BlockSpec(w_out.shape, lambda i: (0, 0)),
                  pl.BlockSpec((1, d_model), lambda i: (0, 0))],
        out_specs=[row_spec, row_spec],
        out_shape=[jax.ShapeDtypeStruct((n_tok, d_model), F32),
                   jax.ShapeDtypeStruct((n_tok, d_model), BF16)],
        compiler_params=_params(("parallel",), 56),
        name="outproj",
    )(attn, ret, x2, w_out, norm_w)


def _ffn_kernel(n_ref, h_ref, wg_ref, wu_ref, wd_ref, fw_ref, o_ref, *,
                final_norm, h_slices):
    f = pl.program_id(1)
    h_rows = h_ref.shape[0]

    @pl.when(f == 0)
    def _():
        o_ref[...] = jnp.zeros_like(o_ref)

    n = n_ref[...]
    g = jnp.dot(n, wg_ref[...], preferred_element_type=F32)
    u = jnp.dot(n, wu_ref[...], preferred_element_type=F32)
    a = (g * (1.0 / (1.0 + jnp.exp(-g))) * u).astype(BF16)
    o_ref[...] += jnp.dot(a, wd_ref[...], preferred_element_type=F32)

    slice_idx = jnp.minimum(f, h_slices - 1)
    rows = pl.ds(pl.multiple_of(slice_idx * h_rows, h_rows), h_rows)
    o_ref[rows, :] += h_ref[...] * (f < h_slices).astype(F32)

    if final_norm:
        @pl.when(f == pl.num_programs(1) - 1)
        def _():
            o_ref[...] = _rmsnorm_rows(o_ref[...], fw_ref[...])


def _ffn(n2, hid, w_gate, w_up, w_down, final_w, final_norm, tm=1024, tf=FFN_TF,
         h_slices=8):
    n_tok, d_model = hid.shape
    f_steps = w_gate.shape[1] // tf
    assert f_steps >= h_slices and tm % h_slices == 0
    row = lambda i, f: (i, 0)
    col_tile = pl.BlockSpec((d_model, tf), lambda i, f: (0, f))
    return pl.pallas_call(
        functools.partial(_ffn_kernel, final_norm=final_norm, h_slices=h_slices),
        grid=(n_tok // tm, f_steps),
        in_specs=[pl.BlockSpec((tm, d_model), row),
                  pl.BlockSpec((tm // h_slices, d_model),
                               lambda i, f: (i * h_slices + jnp.minimum(f, h_slices - 1), 0)),
                  col_tile, col_tile,
                  pl.BlockSpec((tf, d_model), lambda i, f: (f, 0)),
                  pl.BlockSpec((1, d_model), lambda i, f: (0, 0))],
        out_specs=pl.BlockSpec((tm, d_model), row),
        out_shape=jax.ShapeDtypeStruct((n_tok, d_model), F32),
        compiler_params=_params(("parallel", "arbitrary"), 56),
        name="ffn",
    )(n2, hid, w_gate, w_up, w_down, final_w)


def kernel(x, norm_mix_w, w_in, ret_decay_fwd, ret_decay_bwd, ret_norm_w, w_out,
           norm_ffn_w, w_gate, w_up, w_down, norm_final_w):
    batch, seq, d_model = x.shape
    depth = w_in.shape[0]
    assert depth >= 1
    slopes = jnp.exp2(-8.0 * jnp.arange(1, N_HEADS + 1, dtype=F32) / N_HEADS)
    final_w = norm_final_w.reshape(1, d_model)

    hid = x.reshape(batch * seq, d_model)
    for layer in range(depth):
        nw = norm_mix_w[layer].reshape(1, d_model)
        qkv_a, qkvg_r = _inproj(hid, nw, w_in[layer], 3, F32, BF16, batch, seq)
        attn, w_gate_l, w_up_l = _attention(slopes, qkv_a, [w_gate[layer], w_up[layer]])
        ret, w_out_l, w_down_l = _retention(
            ret_decay_fwd[layer], ret_decay_bwd[layer], qkvg_r,
            ret_norm_w[layer].reshape(N_HEADS, 1, HEAD_DIM), w_out[layer], w_down[layer])
        hid, n2 = _outproj(attn, ret, hid, w_out_l, norm_ffn_w[layer].reshape(1, d_model))
        last = layer == depth - 1
        hid = _ffn(n2, hid, w_gate_l, w_up_l, w_down_l, final_w, final_norm=last)
    return hid.reshape(batch, seq, d_model)
```

```python
import functools

import jax
import jax.numpy as jnp
from jax import lax
from jax.experimental import pallas as pl
from jax.experimental.pallas import tpu as pltpu

F32 = jnp.float32
BF16 = jnp.bfloat16

HEAD_DIM = 128
N_HEADS = 8
D_GROUP = N_HEADS * HEAD_DIM
DILATED_PATTERNS = ((128, 1), (512, 4), (2048, 16))
HALF_BAND = 64
RET_BLOCK = 256
EPS = 1e-6
NEG_BIG = -1e30
LOG2_E = 1.4426950408889634
MIB = 1024 * 1024

FFN_TF = 512
ATT_TQ = 128
ATT_TK = ATT_TQ + 2 * HALF_BAND


def _params(semantics, vmem_mib, flags=None):
    return pltpu.CompilerParams(dimension_semantics=semantics,
                                vmem_limit_bytes=vmem_mib * MIB, flags=flags)


def _to_bf16(x, mul=None):
    return (x if mul is None else x * mul).astype(BF16)


def _rmsnorm_rows(x, w):
    return x * lax.rsqrt(jnp.mean(x * x, axis=-1, keepdims=True) + EPS) * w


W_RING = 3


def _inproj_kernel(x_ref, nw_ref, w_hbm, oa_ref, ob_ref, n_scr, w_ring, w_sem, *,
                   groups, groups_a, x_slices, n_tiles):
    i = pl.program_id(0)
    j = pl.program_id(1)
    slice_rows = x_ref.shape[0]
    first_slice_step = groups - x_slices
    n_use = n_tiles * groups
    prefetch = W_RING - 1

    def w_copy(use, slot):
        col = pl.multiple_of(lax.rem(use, groups) * D_GROUP, D_GROUP)
        return pltpu.make_async_copy(w_hbm.at[:, pl.ds(col, D_GROUP)], w_ring.at[slot],
                                     w_sem.at[slot])

    def norm_slice():
        rows = pl.ds(pl.multiple_of((j - first_slice_step) * slice_rows, slice_rows),
                     slice_rows)
        n_scr[i & 1, rows, :] = _rmsnorm_rows(x_ref[...], nw_ref[...]).astype(BF16)

    def project(o_ref, with_norm):
        use = (i - 1) * groups + j
        slot = lax.rem(use, W_RING)
        w_copy(use, slot).wait()

        @pl.when(use + prefetch < n_use)
        def _():
            w_copy(use + prefetch, lax.rem(use + prefetch, W_RING)).start()

        res = jnp.dot(n_scr[(i - 1) & 1], w_ring[slot], preferred_element_type=F32)
        for h in range(N_HEADS):
            o_ref[h] = res[:, h * HEAD_DIM:(h + 1) * HEAD_DIM].astype(o_ref.dtype)
        if with_norm:
            norm_slice()

    @pl.when(j == 0)
    def _():
        ob_ref[...] = jnp.zeros_like(ob_ref)

    @pl.when((i == 0) & (j == 0))
    def _():
        oa_ref[...] = jnp.zeros_like(oa_ref)
        for use in range(prefetch):
            w_copy(use, use % W_RING).start()

    pl.when((i == 0) & (j >= first_slice_step))(norm_slice)
    cases = {}
    for step in range(groups):
        cases.setdefault((step < groups_a, step >= first_slice_step), []).append(step)
    for (to_a, with_norm), steps in cases.items():
        pl.when((i > 0) & (j >= steps[0]) & (j <= steps[-1]))(
            functools.partial(project, oa_ref if to_a else ob_ref, with_norm))


def _inproj(x2, norm_w, w, groups_a, dtype_a, dtype_b, batch, seq, tm=1024, x_slices=4):
    n_tok, d_model = x2.shape
    groups = w.shape[1] // D_GROUP
    groups_b = groups - groups_a
    n_tiles = n_tok // tm
    s_blocks = seq // tm
    assert x_slices <= groups and tm % x_slices == 0

    def out_spec(group_of_j):
        def index(i, j):
            t = jnp.maximum(i - 1, 0)
            g = jnp.where(i == 0, 0, group_of_j(j))
            return (g, t // s_blocks, 0, t % s_blocks, 0)
        return pl.BlockSpec((None, None, N_HEADS, tm, HEAD_DIM), index)

    def out_shape(n_groups, dtype):
        return jax.ShapeDtypeStruct((n_groups, batch, N_HEADS, seq, HEAD_DIM), dtype)

    return pl.pallas_call(
        functools.partial(_inproj_kernel, groups=groups, groups_a=groups_a,
                          x_slices=x_slices, n_tiles=n_tiles),
        grid=(n_tiles + 1, groups),
        in_specs=[
            pl.BlockSpec((tm // x_slices, d_model),
                         lambda i, j: (jnp.minimum(i, n_tiles - 1) * x_slices
                                       + jnp.maximum(j - (groups - x_slices), 0), 0)),
            pl.BlockSpec((1, d_model), lambda i, j: (0, 0)),
            pl.BlockSpec(memory_space=pl.ANY),
        ],
        out_specs=[out_spec(lambda j: jnp.minimum(j, groups_a - 1)),
                   out_spec(lambda j: jnp.maximum(j - groups_a, 0))],
        out_shape=[out_shape(groups_a, dtype_a), out_shape(groups_b, dtype_b)],
        scratch_shapes=[pltpu.VMEM((2, tm, d_model), BF16),
                        pltpu.VMEM((W_RING, d_model, D_GROUP), BF16),
                        pltpu.SemaphoreType.DMA((W_RING,))],
        compiler_params=_params(("arbitrary", "arbitrary"), 48),
        name="inproj",
    )(x2, norm_w, w)


def _attn_kernel(slopes_ref, q_ref, k_ref, v_ref, wg_ref, wu_ref, wd_ref,
                 o_ref, wg_bf, wu_bf, wd_bf,
                 q4f, k4f, v4f,
                 num4, den4, m4, num1, den1, m1, bias_s):
    for w_src, w_dst in ((wg_ref, wg_bf), (wu_ref, wu_bf), (wd_ref, wd_bf)):
        w_dst[...] = w_src[...].astype(BF16)

    seq = q_ref.shape[0]
    quarter = seq // 4
    q_scale = HEAD_DIM ** -0.5 * LOG2_E

    @pl.when(pl.program_id(1) == 0)
    def _():
        slope = slopes_ref[pl.program_id(0)]
        row = lax.broadcasted_iota(jnp.int32, (ATT_TQ, ATT_TK), 0)
        col = lax.broadcasted_iota(jnp.int32, (ATT_TQ, ATT_TK), 1)
        for p, (_, dil) in enumerate(DILATED_PATTERNS):
            for c, off in enumerate((0, HALF_BAND, 2 * HALF_BAND)):
                dist = jnp.abs(row - col + off)
                bias = -slope * (dil * dist).astype(F32) * LOG2_E
                bias_s[p * 3 + c] = jnp.where(dist <= HALF_BAND, bias, NEG_BIG)

    for src, dst in ((q_ref, q4f), (k_ref, k4f), (v_ref, v4f)):
        for r in range(4):
            dst[r * quarter:(r + 1) * quarter, :] = src[pl.ds(r, quarter, stride=4), :]

    def run_pattern(p, dil, sources, rows_of, incoming, emit):
        q_src, k_src, v_src = sources
        sub_len = seq // dil
        n_blk = sub_len // ATT_TQ
        shift = n_blk.bit_length() - 1
        assert n_blk == 1 << shift and n_blk >= 2

        def block(idx, carry):
            r = idx >> shift
            blk = idx & (n_blk - 1)
            q0 = blk * ATT_TQ
            k0 = jnp.clip(q0 - HALF_BAND, 0, sub_len - ATT_TK)
            q_rows = pl.ds(pl.multiple_of(r * sub_len + q0, ATT_TQ), ATT_TQ)
            case = jnp.where(blk == 0, 0, jnp.where(blk == n_blk - 1, 2, 1))
            q = _to_bf16(q_src[rows_of(r, q0, ATT_TQ), :], q_scale)
            k = _to_bf16(k_src[rows_of(r, k0, ATT_TK), :])
            v = _to_bf16(v_src[rows_of(r, k0, ATT_TK), :])
            s = lax.dot_general(q, k, (((1,), (1,)), ((), ())),
                                preferred_element_type=F32)
            s = s + bias_s[p * 3 + case]
            m_blk = jnp.max(s, axis=-1, keepdims=True)
            if incoming is None:
                m_new = jnp.broadcast_to(m_blk, (ATT_TQ, HEAD_DIM))
            else:
                num_in, den_in, m_in = incoming
                m_old = m_in[q_rows, :]
                m_new = jnp.maximum(m_old, m_blk)
            e = jnp.exp2(s - jnp.concatenate([m_new, m_new], axis=-1))
            pv = jnp.dot(e.astype(BF16), jnp.concatenate([v, jnp.ones_like(v)], axis=-1),
                         preferred_element_type=F32)
            num = pv[:, :HEAD_DIM]
            den = pv[:, HEAD_DIM:]
            if incoming is not None:
                alpha = jnp.exp2(m_old - m_new)
                num = alpha * num_in[q_rows, :] + num
                den = alpha * den_in[q_rows, :] + den
            emit(r, q0, num, den, m_new)
            return carry

        lax.fori_loop(0, dil * n_blk, block, 0, unroll=32)

    copies4 = (q4f, k4f, v4f)

    def rows16(r, l0, n):
        return pl.ds((r & 3) * quarter + 4 * l0 + (r >> 2), n, stride=4)

    def emit16(r, q0, num, den, m_new):
        rows = pl.ds((r & 3) * quarter + 4 * q0 + (r >> 2), ATT_TQ, stride=4)
        num4[rows, :] = num
        den4[rows, :] = den
        m4[rows, :] = m_new

    run_pattern(2, 16, copies4, rows16, None, emit16)

    def rows4(r, l0, n):
        return pl.ds(pl.multiple_of(r * quarter + l0, HALF_BAND), n)

    def emit4(r, q0, num, den, m_new):
        rows = pl.ds(4 * q0 + r, ATT_TQ, stride=4)
        num1[rows, :] = num
        den1[rows, :] = den
        m1[rows, :] = m_new

    run_pattern(1, 4, copies4, rows4, (num4, den4, m4), emit4)

    def rows1(r, l0, n):
        return pl.ds(pl.multiple_of(l0, HALF_BAND), n)

    def emit1(r, q0, num, den, m_new):
        o_ref[pl.ds(pl.multiple_of(q0, ATT_TQ), ATT_TQ), :] = (num / den).astype(o_ref.dtype)

    run_pattern(0, 1, (q_ref, k_ref, v_ref), rows1, (num1, den1, m1), emit1)


def _cast_specs(weights, n_steps, step_of):
    specs, shapes = [], []
    for w in weights:
        rows = w.shape[0] // n_steps
        assert rows * n_steps == w.shape[0] and rows % 16 == 0
        specs.append(pl.BlockSpec((rows, w.shape[1]), lambda *idx: (step_of(*idx), 0)))
        shapes.append(jax.ShapeDtypeStruct(w.shape, BF16))
    return specs, shapes


def _attention(slopes, qkv, ffn_weights):
    _, batch, heads, seq, dh = qkv.shape
    assert seq % (16 * ATT_TK) == 0 and dh == HEAD_DIM

    def spec(t):
        return pl.BlockSpec((None, None, None, seq, dh),
                            lambda h, b, t=t: (t, b, h, 0, 0))

    w_specs, w_shapes = _cast_specs(ffn_weights, heads * batch, lambda h, b: h * batch + b)
    f32_rows = pltpu.VMEM((seq, dh), F32)
    return pl.pallas_call(
        _attn_kernel,
        grid=(heads, batch),
        in_specs=[pl.BlockSpec(memory_space=pltpu.SMEM), spec(0), spec(1), spec(2)] + w_specs,
        out_specs=[pl.BlockSpec((None, None, seq, dh), lambda h, b: (b, h, 0, 0))] + w_specs,
        out_shape=[jax.ShapeDtypeStruct((batch, heads, seq, dh), BF16)] + w_shapes,
        scratch_shapes=[
            f32_rows, f32_rows, f32_rows,
            f32_rows, f32_rows, f32_rows, f32_rows, f32_rows, f32_rows,
            pltpu.VMEM((3 * len(DILATED_PATTERNS), ATT_TQ, ATT_TK), F32),
        ],
        compiler_params=_params(("parallel", "arbitrary"), 56),
        name="dilated_attention",
    )(slopes, qkv, qkv, qkv, *ffn_weights)


def _ret_kernel(df_ref, db_ref, q_ref, k_ref, v_ref, g_ref, rw_ref, wo_ref,
                o_ref, wo_bf, intra_s, tab, kv_s, st_s):
    wo_bf[...] = wo_ref[...].astype(BF16)

    seq = q_ref.shape[0]
    ch = RET_BLOCK
    n_ch = seq // ch
    h = pl.program_id(0)
    scale = HEAD_DIM ** -0.5
    T_QF, T_QB, T_KF, T_KB = range(4)

    lg_f = -jnp.exp(jnp.full((1, HEAD_DIM), df_ref[h], F32))
    lg_b = -jnp.exp(jnp.full((1, HEAD_DIM), db_ref[h], F32))

    @pl.when(pl.program_id(1) == 0)
    def _():
        rel = (lax.broadcasted_iota(jnp.int32, (ch, ch), 0)
               - lax.broadcasted_iota(jnp.int32, (ch, ch), 1)).astype(F32)
        intra_s[...] = jnp.where(rel >= 0.0,
                                 jnp.exp(lg_f[:, :1] * jnp.maximum(rel, 0.0)),
                                 jnp.exp(lg_b[:, :1] * jnp.maximum(-rel, 0.0))) * scale
        ri = lax.broadcasted_iota(jnp.int32, (ch, HEAD_DIM), 0).astype(F32)
        tab[T_QF] = jnp.exp(lg_f * (ri + 1.0)) * scale
        tab[T_QB] = jnp.exp(lg_b * (ch - ri)) * scale
        tab[T_KF] = jnp.exp(lg_f * (ch - 1.0 - ri))
        tab[T_KB] = jnp.exp(lg_b * ri)

    cdec_f = jnp.exp(lg_f * ch)
    cdec_b = jnp.exp(lg_b * ch)

    def chunk_rows(c):
        return pl.ds(pl.multiple_of(c * ch, ch), ch)

    def kv_step(c, carry):
        rows = chunk_rows(c)
        vc = v_ref[rows, :].astype(F32)
        v2 = jnp.concatenate([vc * tab[T_KF], vc * tab[T_KB]], axis=-1).astype(BF16)
        kv_s[c] = lax.dot_general(k_ref[rows, :], v2, (((0,), (0,)), ((), ())),
                                  preferred_element_type=F32)
        return carry

    lax.fori_loop(0, n_ch, kv_step, 0, unroll=n_ch)
    zero_state = jnp.zeros((HEAD_DIM, HEAD_DIM), F32)

    def scan_f(c, state):
        st_s[c, :, :HEAD_DIM] = state.astype(BF16)
        return state * cdec_f + kv_s[c, :, :HEAD_DIM]

    def scan_b(i, state):
        c = n_ch - 1 - i
        st_s[c, :, HEAD_DIM:] = state.astype(BF16)
        return state * cdec_b + kv_s[c, :, HEAD_DIM:]

    lax.fori_loop(0, n_ch, scan_f, zero_state, unroll=n_ch)
    lax.fori_loop(0, n_ch, scan_b, zero_state, unroll=n_ch)

    def out_step(c, carry):
        rows = chunk_rows(c)
        qb = q_ref[rows, :]
        inner = lax.dot_general(qb, k_ref[rows, :], (((1,), (1,)), ((), ())),
                                preferred_element_type=F32) * intra_s[...]
        o = jnp.dot(inner.astype(BF16), v_ref[rows, :], preferred_element_type=F32)
        cross = jnp.dot(qb, st_s[c], preferred_element_type=F32)
        o = o + cross[:, :HEAD_DIM] * tab[T_QF] + cross[:, HEAD_DIM:] * tab[T_QB]
        o = o * lax.rsqrt(jnp.mean(o * o, axis=-1, keepdims=True) + EPS)
        o = o * rw_ref[...]
        g = g_ref[rows, :].astype(F32)
        o = o * (g * (1.0 / (1.0 + jnp.exp(-g))))
        o_ref[rows, :] = o.astype(o_ref.dtype)
        return carry

    lax.fori_loop(0, n_ch, out_step, 0, unroll=n_ch)


def _retention(decay_f, decay_b, qkvg, ret_norm_w, w_out):
    _, batch, heads, seq, dh = qkvg.shape

    def spec(t):
        return pl.BlockSpec((None, None, None, seq, dh),
                            lambda h, b, t=t: (t, b, h, 0, 0))

    smem = pl.BlockSpec(memory_space=pltpu.SMEM)
    w_specs, w_shapes = _cast_specs([w_out], heads * batch, lambda h, b: h * batch + b)
    return pl.pallas_call(
        _ret_kernel,
        grid=(heads, batch),
        in_specs=[smem, smem, spec(0), spec(1), spec(2), spec(3),
                  pl.BlockSpec((None, 1, dh), lambda h, b: (h, 0, 0))] + w_specs,
        out_specs=[pl.BlockSpec((None, None, seq, dh), lambda h, b: (b, h, 0, 0))] + w_specs,
        out_shape=[jax.ShapeDtypeStruct((batch, heads, seq, dh), BF16)] + w_shapes,
        scratch_shapes=[pltpu.VMEM((RET_BLOCK, RET_BLOCK), F32),
                        pltpu.VMEM((4, RET_BLOCK, dh), F32),
                        pltpu.VMEM((seq // RET_BLOCK, dh, 2 * dh), F32),
                        pltpu.VMEM((seq // RET_BLOCK, dh, 2 * dh), BF16)],
        compiler_params=_params(("parallel", "arbitrary"), 32),
        name="retention",
    )(decay_f, decay_b, qkvg, qkvg, qkvg, qkvg, ret_norm_w, w_out)


def _outproj_kernel(a_ref, r_ref, x_ref, w_ref, nw_ref, h_ref, n_ref):
    mixed = jnp.concatenate([a_ref[h] for h in range(N_HEADS)]
                            + [r_ref[h] for h in range(N_HEADS)], axis=-1)
    hid = x_ref[...] + jnp.dot(mixed, w_ref[...], preferred_element_type=F32)
    h_ref[...] = hid
    n_ref[...] = _rmsnorm_rows(hid, nw_ref[...]).astype(n_ref.dtype)


def _outproj(attn, ret, x2, w_out, norm_w, tm=512):
    batch, heads, seq, dh = attn.shape
    n_tok, d_model = x2.shape
    s_blocks = seq // tm
    head_spec = pl.BlockSpec((None, heads, tm, dh),
                             lambda i: (i // s_blocks, 0, i % s_blocks, 0))
    row_spec = pl.BlockSpec((tm, d_model), lambda i: (i, 0))
    return pl.pallas_call(
        _outproj_kernel,
        grid=(n_tok // tm,),
        in_specs=[head_spec, head_spec, row_spec,
                  pl.BlockSpec(w_out.shape, lambda i: (0, 0)),
                  pl.BlockSpec((1, d_model), lambda i: (0, 0))],
        out_specs=[row_spec, row_spec],
        out_shape=[jax.ShapeDtypeStruct((n_tok, d_model), F32),
                   jax.ShapeDtypeStruct((n_tok, d_model), BF16)],
        compiler_params=_params(("parallel",), 56),
        name="outproj",
    )(attn, ret, x2, w_out, norm_w)


def _ffn_kernel(n_ref, h_ref, wg_ref, wu_ref, wd_ref, fw_ref, o_ref, *,
                final_norm, h_slices):
    f = pl.program_id(1)
    h_rows = h_ref.shape[0]

    @pl.when(f == 0)
    def _():
        o_ref[...] = jnp.zeros_like(o_ref)

    n = n_ref[...]
    g = jnp.dot(n, wg_ref[...], preferred_element_type=F32)
    u = jnp.dot(n, wu_ref[...], preferred_element_type=F32)
    a = (g * (1.0 / (1.0 + jnp.exp(-g))) * u).astype(BF16)
    o_ref[...] += jnp.dot(a, wd_ref[...], preferred_element_type=F32)

    slice_idx = jnp.minimum(f, h_slices - 1)
    rows = pl.ds(pl.multiple_of(slice_idx * h_rows, h_rows), h_rows)
    o_ref[rows, :] += h_ref[...] * (f < h_slices).astype(F32)

    if final_norm:
        @pl.when(f == pl.num_programs(1) - 1)
        def _():
            o_ref[...] = _rmsnorm_rows(o_ref[...], fw_ref[...])


def _ffn(n2, hid, w_gate, w_up, w_down, final_w, final_norm, tm=1024, tf=FFN_TF,
         h_slices=8):
    n_tok, d_model = hid.shape
    f_steps = w_gate.shape[1] // tf
    assert f_steps >= h_slices and tm % h_slices == 0
    row = lambda i, f: (i, 0)
    col_tile = pl.BlockSpec((d_model, tf), lambda i, f: (0, f))
    return pl.pallas_call(
        functools.partial(_ffn_kernel, final_norm=final_norm, h_slices=h_slices),
        grid=(n_tok // tm, f_steps),
        in_specs=[pl.BlockSpec((tm, d_model), row),
                  pl.BlockSpec((tm // h_slices, d_model),
                               lambda i, f: (i * h_slices + jnp.minimum(f, h_slices - 1), 0)),
                  col_tile, col_tile,
                  pl.BlockSpec((tf, d_model), lambda i, f: (f, 0)),
                  pl.BlockSpec((1, d_model), lambda i, f: (0, 0))],
        out_specs=pl.BlockSpec((tm, d_model), row),
        out_shape=jax.ShapeDtypeStruct((n_tok, d_model), F32),
        compiler_params=_params(("parallel", "arbitrary"), 56),
        name="ffn",
    )(n2, hid, w_gate, w_up, w_down, final_w)


def kernel(x, norm_mix_w, w_in, ret_decay_fwd, ret_decay_bwd, ret_norm_w, w_out,
           norm_ffn_w, w_gate, w_up, w_down, norm_final_w):
    batch, seq, d_model = x.shape
    depth = w_in.shape[0]
    assert depth >= 1
    slopes = jnp.exp2(-8.0 * jnp.arange(1, N_HEADS + 1, dtype=F32) / N_HEADS)
    final_w = norm_final_w.reshape(1, d_model)

    hid = x.reshape(batch * seq, d_model)
    for layer in range(depth):
        w_in_l = w_in[layer].astype(BF16)
        nw = norm_mix_w[layer].reshape(1, d_model)
        qkv_a, qkvg_r = _inproj(hid, nw, w_in_l, 3, F32, BF16, batch, seq)
        attn, w_gate_l, w_up_l, w_down_l = _attention(
            slopes, qkv_a, [w_gate[layer], w_up[layer], w_down[layer]])
        ret, w_out_l = _retention(ret_decay_fwd[layer], ret_decay_bwd[layer], qkvg_r,
                                  ret_norm_w[layer].reshape(N_HEADS, 1, HEAD_DIM),
                                  w_out[layer])
        hid, n2 = _outproj(attn, ret, hid, w_out_l, norm_ffn_w[layer].reshape(1, d_model))
        last = layer == depth - 1
        hid = _ffn(n2, hid, w_gate_l, w_up_l, w_down_l, final_w, final_norm=last)
    return hid.reshape(batch, seq, d_model)
```
